```python
import jax, jax.numpy as jnp
from jax import lax
import numpy as np

D_MODEL = 1024
BATCH = 16
SEQ = 4096
DEPTH = 1
DEC_BATCH = 1
DEC_SEQ = 16384
PAST_LEN = 128

D_RNN = 1024
N_RNN_BLOCKS = 8
RNN_BLOCK = D_RNN // N_RNN_BLOCKS
RNN_CONV_W = 4
LRU_C = 8.0
D_CONV = 1024
CONF_CONV_W = 31
D_FF = -(-8 * D_MODEL // (3 * 256)) * 256
IN_COLS = 2 * D_RNN + 2 * D_CONV + 2 * D_MODEL
LN_EPS = 1e-5
DEEPNORM_ALPHA = (2.0 * DEPTH) ** 0.25
DEEPNORM_BETA = (8.0 * DEPTH) ** -0.25

kernel_name = "hybrid_rglru_conformer_encoder"


def layer_norm(x, g, b):
    xf = x.astype(jnp.float32)
    mu = jnp.mean(xf, axis=-1, keepdims=True)
    var = jnp.mean(jnp.square(xf - mu), axis=-1, keepdims=True)
    y = (xf - mu) * lax.rsqrt(var + LN_EPS) * g.astype(jnp.float32) + b.astype(jnp.float32)
    return y.astype(x.dtype)


def depthwise_conv(x, w, b, pad_left, pad_right):
    c = x.shape[-1]
    y = lax.conv_general_dilated(
        x, w.astype(x.dtype)[:, None, :], window_strides=(1,),
        padding=[(pad_left, pad_right)],
        dimension_numbers=("NWC", "WIO", "NWC"), feature_group_count=c)
    return y + b.astype(x.dtype)


def linear_recurrence(a, b, reverse):
    def combine(left, right):
        a_l, b_l = left
        a_r, b_r = right
        return a_l * a_r, a_r * b_l + b_r
    _, h = lax.associative_scan(combine, (a, b), axis=1, reverse=reverse)
    return h


def rg_lru(xc, wa, ba, wx, bx, lam, reverse):
    bsz, s, _ = xc.shape
    xb = xc.reshape(bsz, s, N_RNN_BLOCKS, RNN_BLOCK)
    r = jax.nn.sigmoid(jnp.einsum("bshi,hij->bshj", xb, wa.astype(jnp.float32)).reshape(bsz, s, D_RNN)
                       + ba.astype(jnp.float32))
    i = jax.nn.sigmoid(jnp.einsum("bshi,hij->bshj", xb, wx.astype(jnp.float32)).reshape(bsz, s, D_RNN)
                       + bx.astype(jnp.float32))
    log_a = -LRU_C * r * jax.nn.softplus(-lam.astype(jnp.float32))
    a = jnp.exp(log_a)
    b = jnp.sqrt(-jnp.expm1(2.0 * log_a)) * (i * xc)
    return linear_recurrence(a, b, reverse)


def encoder_layer(x, w_in, rnn_conv_w, rnn_conv_b, lru_wa, lru_ba, lru_wx, lru_bx, lru_lambda,
                  w_rnn_proj, conf_conv_w, conf_conv_b, conf_norm_g, conf_norm_b, w_conf_proj,
                  gate_b, w_out, ln1_g, ln1_b, w_ffn_in, w_ffn_out, ln2_g, ln2_b):
    bsz, s, _ = x.shape
    proj = x @ w_in
    x_rnn, g_rnn, conf_u, conf_v, gate_logits = jnp.split(
        proj, [D_RNN, 2 * D_RNN, 2 * D_RNN + D_CONV, 2 * D_RNN + 2 * D_CONV], axis=-1)

    pad_l = RNN_CONV_W // 2
    xc = depthwise_conv(x_rnn, rnn_conv_w, rnn_conv_b, pad_l, RNN_CONV_W - 1 - pad_l).astype(jnp.float32)
    h = (rg_lru(xc, lru_wa[0], lru_ba[0], lru_wx[0], lru_bx[0], lru_lambda[0], reverse=False)
         + rg_lru(xc, lru_wa[1], lru_ba[1], lru_wx[1], lru_bx[1], lru_lambda[1], reverse=True))
    y_a = (h.astype(x.dtype) * jax.nn.gelu(g_rnn)) @ w_rnn_proj

    u = conf_u * jax.nn.sigmoid(conf_v)
    pad_c = CONF_CONV_W // 2
    u = depthwise_conv(u, conf_conv_w, conf_conv_b, pad_c, CONF_CONV_W - 1 - pad_c)
    u = jax.nn.silu(layer_norm(u, conf_norm_g, conf_norm_b))
    y_b = u @ w_conf_proj

    gates = jax.nn.sigmoid(gate_logits.reshape(bsz, s, 2, D_MODEL) + gate_b)
    merged = gates[:, :, 0, :] * y_a + gates[:, :, 1, :] * y_b
    x = layer_norm(DEEPNORM_ALPHA * x + merged @ w_out, ln1_g, ln1_b)

    g_ff, up = jnp.split(x @ w_ffn_in, [D_FF], axis=-1)
    x = layer_norm(DEEPNORM_ALPHA * x + (jax.nn.silu(g_ff) * up) @ w_ffn_out, ln2_g, ln2_b)
    return x


def trunk(x, weights):
    for l in range(DEPTH):
        x = encoder_layer(x, *[w[l] for w in weights])
    return x


def setup_inputs(seed: int = 0) -> dict:
    key = jax.random.key(seed)
    ks = jax.random.split(key, 24)
    f32 = jnp.float32

    def nrm(k, shape, scale):
        return jax.random.normal(k, shape, f32) * scale

    a0 = jax.random.uniform(ks[8], (DEPTH, 2, D_RNN), f32, 0.9, 0.999)
    p = a0 ** (1.0 / LRU_C)
    lru_lambda = jnp.log(p) - jnp.log1p(-p)
    return {
        "x_prompt": nrm(ks[0], (BATCH, SEQ, D_MODEL), 1.0),
        "x_sample": nrm(ks[1], (DEC_BATCH, DEC_SEQ, D_MODEL), 1.0),
        "w_in": nrm(ks[2], (DEPTH, D_MODEL, IN_COLS), D_MODEL ** -0.5),
        "rnn_conv_w": nrm(ks[3], (DEPTH, RNN_CONV_W, D_RNN), RNN_CONV_W ** -0.5),
        "rnn_conv_b": nrm(ks[4], (DEPTH, D_RNN), 0.01),
        "lru_wa": nrm(ks[5], (DEPTH, 2, N_RNN_BLOCKS, RNN_BLOCK, RNN_BLOCK), RNN_BLOCK ** -0.5),
        "lru_ba": nrm(ks[6], (DEPTH, 2, D_RNN), 0.01),
        "lru_wx": nrm(ks[7], (DEPTH, 2, N_RNN_BLOCKS, RNN_BLOCK, RNN_BLOCK), RNN_BLOCK ** -0.5),
        "lru_bx": nrm(ks[9], (DEPTH, 2, D_RNN), 0.01),
        "lru_lambda": lru_lambda,
        "w_rnn_proj": nrm(ks[10], (DEPTH, D_RNN, D_MODEL), D_RNN ** -0.5),
        "conf_conv_w": nrm(ks[11], (DEPTH, CONF_CONV_W, D_CONV), CONF_CONV_W ** -0.5),
        "conf_conv_b": nrm(ks[12], (DEPTH, D_CONV), 0.01),
        "conf_norm_g": 1.0 + nrm(ks[13], (DEPTH, D_CONV), 0.02),
        "conf_norm_b": nrm(ks[14], (DEPTH, D_CONV), 0.01),
        "w_conf_proj": nrm(ks[15], (DEPTH, D_CONV, D_MODEL), D_CONV ** -0.5),
        "gate_b": nrm(ks[16], (DEPTH, 2, D_MODEL), 0.01),
        "w_out": nrm(ks[17], (DEPTH, D_MODEL, D_MODEL), D_MODEL ** -0.5 * DEEPNORM_BETA),
        "ln1_g": 1.0 + nrm(ks[18], (DEPTH, D_MODEL), 0.02),
        "ln1_b": nrm(ks[19], (DEPTH, D_MODEL), 0.01),
        "w_ffn_in": nrm(ks[20], (DEPTH, D_MODEL, 2 * D_FF), D_MODEL ** -0.5),
        "w_ffn_out": nrm(ks[21], (DEPTH, D_FF, D_MODEL), D_FF ** -0.5 * DEEPNORM_BETA),
        "ln2_g": 1.0 + nrm(ks[22], (DEPTH, D_MODEL), 0.02),
        "ln2_b": nrm(ks[23], (DEPTH, D_MODEL), 0.01),
    }


def reference(x_prompt, x_sample, w_in, rnn_conv_w, rnn_conv_b, lru_wa, lru_ba, lru_wx, lru_bx,
              lru_lambda, w_rnn_proj, conf_conv_w, conf_conv_b, conf_norm_g, conf_norm_b, w_conf_proj,
              gate_b, w_out, ln1_g, ln1_b, w_ffn_in, w_ffn_out, ln2_g, ln2_b):
    weights = (w_in, rnn_conv_w, rnn_conv_b, lru_wa, lru_ba, lru_wx, lru_bx, lru_lambda,
               w_rnn_proj, conf_conv_w, conf_conv_b, conf_norm_g, conf_norm_b, w_conf_proj,
               gate_b, w_out, ln1_g, ln1_b, w_ffn_in, w_ffn_out, ln2_g, ln2_b)
    y_prompt = trunk(x_prompt, weights)
    y_sample = trunk(x_sample, weights)
    return (y_prompt, y_sample)
```

```python
import functools

import jax
import jax.numpy as jnp
from jax import lax
from jax.experimental import pallas as pl
from jax.experimental.pallas import tpu as pltpu

D_MODEL = 1024
N_HEADS = 8
HEAD = D_MODEL // N_HEADS
RNN_CONV_W = 4
CONF_CONV_W = 31
LRU_C = 8.0
LN_EPS = 1e-5
D_FF = 2816
FF_CHUNK = 256

SUBLANES = 8
LANES = 128
LANE_BLOCKS = D_MODEL // LANES

RNN_HALO = 8
CONF_HALO = 16
TIME_TILE = 256
FFN_ROWS = 512
VMEM_LIMIT_BYTES = 56 * 1024 * 1024

F32 = jnp.float32
BF16 = jnp.bfloat16


def _const_spec(shape):
    n = len(shape)
    return pl.BlockSpec(shape, lambda *_: (0,) * n, pipeline_mode=pl.Buffered(1))


def _to_rows(dst_ref, val):
    rows = val.shape[0]
    for c in range(LANE_BLOCKS):
        dst_ref[pl.ds(c, rows, stride=LANE_BLOCKS), :] = val[:, c * LANES:(c + 1) * LANES]


def _from_rows(src_ref, rows):
    cols = [src_ref[pl.ds(c, rows, stride=LANE_BLOCKS), :] for c in range(LANE_BLOCKS)]
    return jnp.concatenate(cols, axis=1)


def _row(ref, t):
    return ref[pl.ds(pl.multiple_of(t * SUBLANES, SUBLANES), SUBLANES), :]


def _set_row(ref, t, val):
    ref[pl.ds(pl.multiple_of(t * SUBLANES, SUBLANES), SUBLANES), :] = val


def _depthwise_conv_rows(src_ref, w_ref, b_ref, dst_ref, n_out, width, src_offset, block):
    bias = b_ref[...]

    def body(i, carry):
        t0 = i * block
        for tt in range(block):
            acc = bias
            for k in range(width):
                acc = acc + w_ref[k] * _row(src_ref, t0 + tt + src_offset + k)
            _set_row(dst_ref, t0 + tt, acc)
        return carry

    lax.fori_loop(0, n_out // block, body, 0)


def _softplus(z):
    return jnp.maximum(z, 0.0) + jnp.log1p(jnp.exp(-jnp.abs(z)))


def _lru_gates(xc, wg_ref, ba_ref, bx_ref, lam_ref, a_rows, b_rows):
    rows = xc.shape[0]
    for h in range(N_HEADS):
        sl = slice(h * HEAD, (h + 1) * HEAD)
        xh = xc[:, sl]
        z = jnp.dot(xh.astype(BF16), wg_ref[h], preferred_element_type=F32)
        r = jax.nn.sigmoid(z[:, :HEAD] + ba_ref[:, sl])
        i = jax.nn.sigmoid(z[:, HEAD:] + bx_ref[:, sl])
        log_a = (-LRU_C * r) * _softplus(-lam_ref[:, sl])
        a = jnp.exp(log_a)
        b = jnp.sqrt(1.0 - a * a) * (i * xh)
        a_rows[pl.ds(h, rows, stride=LANE_BLOCKS), :] = a
        b_rows[pl.ds(h, rows, stride=LANE_BLOCKS), :] = b


def _layer_norm(x, g, b):
    mu = jnp.mean(x, axis=-1, keepdims=True)
    xm = x - mu
    var = jnp.mean(xm * xm, axis=-1, keepdims=True)
    return xm * lax.rsqrt(var + LN_EPS) * g + b


def _halo_tile(x_ref, prev_ref, next_ref, is_first, is_last):
    xp = jnp.where(is_first, 0.0, prev_ref[...])
    xn = jnp.where(is_last, 0.0, next_ref[...])
    return jnp.concatenate([xp, x_ref[...], xn], axis=0)


def _bwd_pass_kernel(x_ref, xprev_ref, xnext_ref, wxr_ref, cw_ref, cb_ref, wg_ref, ba_ref, bx_ref,
                     lam_ref, xc_ref, hb_ref, xr_rows, xc_rows, a_rows, b_rows, h_state):
    j = pl.program_id(1)
    n_t = pl.num_programs(1)
    jr = n_t - 1 - j
    rows = x_ref.shape[0]

    xh = _halo_tile(x_ref, xprev_ref, xnext_ref, jr == 0, jr == n_t - 1).astype(BF16)
    xr = jnp.dot(xh, wxr_ref[...], preferred_element_type=F32)
    _to_rows(xr_rows, xr)
    pad_left = RNN_CONV_W // 2
    _depthwise_conv_rows(xr_rows, cw_ref, cb_ref, xc_rows, rows, RNN_CONV_W, RNN_HALO - pad_left, 8)
    xc = _from_rows(xc_rows, rows)
    xc_ref[...] = xc

    _lru_gates(xc, wg_ref, ba_ref, bx_ref, lam_ref, a_rows, b_rows)

    h0 = jnp.where(j == 0, 0.0, h_state[...])

    def body(i, h):
        t = rows - 1 - i
        h = _row(a_rows, t) * h + _row(b_rows, t)
        _set_row(hb_ref, t, h)
        return h

    h_state[...] = lax.fori_loop(0, rows, body, h0, unroll=8)


def _fwd_pass_kernel(x_ref, xprev_ref, xnext_ref, xc_ref, hb_ref,
                     wg_rnn_ref, wconf_ref, wgate_ref,
                     wlru_ref, ba_ref, bx_ref, lam_ref, wrp_ref,
                     ccw_ref, ccb_ref, cng_ref, cnb_ref, wcp_ref,
                     gb_ref, wout_ref, ln_g_ref, ln_b_ref,
                     out_ref, u_rows, y_rows, a_rows, b_rows, h_rows, h_state, *, alpha):
    j = pl.program_id(1)
    n_t = pl.num_programs(1)
    rows = x_ref.shape[0]

    xh = _halo_tile(x_ref, xprev_ref, xnext_ref, j == 0, j == n_t - 1).astype(BF16)
    xb = xh[CONF_HALO:CONF_HALO + rows]

    uv = jnp.dot(xh, wconf_ref[...], preferred_element_type=F32)
    u = uv[:, :D_MODEL] * jax.nn.sigmoid(uv[:, D_MODEL:])
    _to_rows(u_rows, u)
    pad_c = CONF_CONV_W // 2
    _depthwise_conv_rows(u_rows, ccw_ref, ccb_ref, y_rows, rows, CONF_CONV_W, CONF_HALO - pad_c, 8)
    y = _from_rows(y_rows, rows)
    un = jax.nn.silu(_layer_norm(y, cng_ref[...], cnb_ref[...]))
    y_b = jnp.dot(un.astype(BF16), wcp_ref[...], preferred_element_type=F32)

    xc = xc_ref[...]
    _lru_gates(xc, wlru_ref, ba_ref, bx_ref, lam_ref, a_rows, b_rows)
    h0 = jnp.where(j == 0, 0.0, h_state[...])

    def body(t, h):
        h = _row(a_rows, t) * h + _row(b_rows, t)
        _set_row(h_rows, t, h + _row(hb_ref, t))
        return h

    h_state[...] = lax.fori_loop(0, rows, body, h0, unroll=8)
    h_sum = _from_rows(h_rows, rows)
    g_rnn = jnp.dot(xb, wg_rnn_ref[...], preferred_element_type=F32)
    y_a = jnp.dot((h_sum * jax.nn.gelu(g_rnn)).astype(BF16), wrp_ref[...], preferred_element_type=F32)

    gl = jnp.dot(xb, wgate_ref[...], preferred_element_type=F32)
    g0 = jax.nn.sigmoid(gl[:, :D_MODEL] + gb_ref[0:1, :])
    g1 = jax.nn.sigmoid(gl[:, D_MODEL:] + gb_ref[1:2, :])
    merged = g0 * y_a + g1 * y_b
    mo = jnp.dot(merged.astype(BF16), wout_ref[...], preferred_element_type=F32)
    out_ref[...] = _layer_norm(alpha * x_ref[...] + mo, ln_g_ref[...], ln_b_ref[...])


def _ffn_kernel(x_ref, win_ref, wout_ref, ln_g_ref, ln_b_ref, out_ref, *, alpha):
    x = x_ref[...]
    xb = x.astype(BF16)
    acc = jnp.zeros(x.shape, F32)
    for c in range(D_FF // FF_CHUNK):
        lo = c * FF_CHUNK
        g = jnp.dot(xb, win_ref[:, lo:lo + FF_CHUNK], preferred_element_type=F32)
        up = jnp.dot(xb, win_ref[:, D_FF + lo:D_FF + lo + FF_CHUNK], preferred_element_type=F32)
        hid = (jax.nn.silu(g) * up).astype(BF16)
        acc = acc + jnp.dot(hid, wout_ref[lo:lo + FF_CHUNK, :], preferred_element_type=F32)
    out_ref[...] = _layer_norm(alpha * x + acc, ln_g_ref[...], ln_b_ref[...])


def _compiler_params(n_grid):
    return pltpu.CompilerParams(dimension_semantics=("arbitrary",) * n_grid,
                                vmem_limit_bytes=VMEM_LIMIT_BYTES)


def _halo_specs(rows, halo, seq, time_index):
    per_tile = rows // halo
    last = seq // halo - 1
    main = pl.BlockSpec((None, rows, D_MODEL), lambda b, j: (b, time_index(j), 0))
    prev = pl.BlockSpec((None, halo, D_MODEL),
                        lambda b, j: (b, jnp.maximum(time_index(j) * per_tile - 1, 0), 0))
    nxt = pl.BlockSpec((None, halo, D_MODEL),
                       lambda b, j: (b, jnp.minimum((time_index(j) + 1) * per_tile, last), 0))
    return main, prev, nxt


def _bwd_pass(x, p):
    bsz, seq, _ = x.shape
    rows = TIME_TILE
    n_t = seq // rows
    rev = lambda j: n_t - 1 - j
    main, prev, nxt = _halo_specs(rows, RNN_HALO, seq, rev)
    vec = _const_spec((1, D_MODEL))
    return pl.pallas_call(
        _bwd_pass_kernel,
        grid=(bsz, n_t),
        in_specs=[main, prev, nxt,
                  _const_spec((D_MODEL, D_MODEL)),
                  _const_spec((RNN_CONV_W, SUBLANES, LANES)), _const_spec((SUBLANES, LANES)),
                  _const_spec((N_HEADS, HEAD, 2 * HEAD)), vec, vec, vec],
        out_specs=[pl.BlockSpec((None, rows, D_MODEL), lambda b, j: (b, rev(j), 0)),
                   pl.BlockSpec((None, rows * SUBLANES, LANES), lambda b, j: (b, rev(j), 0))],
        out_shape=[jax.ShapeDtypeStruct((bsz, seq, D_MODEL), F32),
                   jax.ShapeDtypeStruct((bsz, seq * SUBLANES, LANES), F32)],
        scratch_shapes=[pltpu.VMEM(((rows + 2 * RNN_HALO) * SUBLANES, LANES), F32),
                        pltpu.VMEM((rows * SUBLANES, LANES), F32),
                        pltpu.VMEM((rows * SUBLANES, LANES), F32),
                        pltpu.VMEM((rows * SUBLANES, LANES), F32),
                        pltpu.VMEM((SUBLANES, LANES), F32)],
        compiler_params=_compiler_params(2),
        name="lru_bwd_pass",
    )(x, x, x, p["w_xr"], p["rnn_cw"], p["rnn_cb"], p["wlru_b"], p["ba_b"], p["bx_b"], p["lam_b"])


def _fwd_pass(x, xc, hb, p, alpha):
    bsz, seq, _ = x.shape
    rows = TIME_TILE
    n_t = seq // rows
    main, prev, nxt = _halo_specs(rows, CONF_HALO, seq, lambda j: j)
    tile = pl.BlockSpec((None, rows, D_MODEL), lambda b, j: (b, j, 0))
    vec = _const_spec((1, D_MODEL))
    sq = _const_spec((D_MODEL, D_MODEL))
    wide = _const_spec((D_MODEL, 2 * D_MODEL))
    return pl.pallas_call(
        functools.partial(_fwd_pass_kernel, alpha=alpha),
        grid=(bsz, n_t),
        in_specs=[main, prev, nxt, tile,
                  pl.BlockSpec((None, rows * SUBLANES, LANES), lambda b, j: (b, j, 0)),
                  sq, wide, wide,
                  _const_spec((N_HEADS, HEAD, 2 * HEAD)), vec, vec, vec, sq,
                  _const_spec((CONF_CONV_W, SUBLANES, LANES)), _const_spec((SUBLANES, LANES)),
                  vec, vec, sq,
                  _const_spec((2, D_MODEL)), sq, vec, vec],
        out_specs=tile,
        out_shape=jax.ShapeDtypeStruct((bsz, seq, D_MODEL), F32),
        scratch_shapes=[pltpu.VMEM(((rows + 2 * CONF_HALO) * SUBLANES, LANES), F32),
                        pltpu.VMEM((rows * SUBLANES, LANES), F32),
                        pltpu.VMEM((rows * SUBLANES, LANES), F32),
                        pltpu.VMEM((rows * SUBLANES, LANES), F32),
                        pltpu.VMEM((rows * SUBLANES, LANES), F32),
                        pltpu.VMEM((SUBLANES, LANES), F32)],
        compiler_params=_compiler_params(2),
        name="mixer_fwd_pass",
    )(x, x, x, xc, hb,
      p["w_grnn"], p["w_conf"], p["w_gate"],
      p["wlru_f"], p["ba_f"], p["bx_f"], p["lam_f"], p["w_rnn_proj"],
      p["conf_cw"], p["conf_cb"], p["conf_ng"], p["conf_nb"], p["w_conf_proj"],
      p["gate_b"], p["w_out"], p["ln1_g"], p["ln1_b"])


def _ffn_pass(x, p, alpha):
    bsz, seq, _ = x.shape
    x2 = x.reshape(bsz * seq, D_MODEL)
    rows = FFN_ROWS
    tile = pl.BlockSpec((rows, D_MODEL), lambda i: (i, 0))
    vec = _const_spec((1, D_MODEL))
    out = pl.pallas_call(
        functools.partial(_ffn_kernel, alpha=alpha),
        grid=(bsz * seq // rows,),
        in_specs=[tile, _const_spec((D_MODEL, 2 * D_FF)), _const_spec((D_FF, D_MODEL)), vec, vec],
        out_specs=tile,
        out_shape=jax.ShapeDtypeStruct((bsz * seq, D_MODEL), F32),
        compiler_params=_compiler_params(1),
        name="ffn_pass",
    )(x2, p["w_ffn_in"], p["w_ffn_out"], p["ln2_g"], p["ln2_b"])
    return out.reshape(bsz, seq, D_MODEL)


def _layer_params(l, w_in, rnn_conv_w, rnn_conv_b, lru_wa, lru_ba, lru_wx, lru_bx, lru_lambda,
                  w_rnn_proj, conf_conv_w, conf_conv_b, conf_norm_g, conf_norm_b, w_conf_proj,
                  gate_b, w_out, ln1_g, ln1_b, w_ffn_in, w_ffn_out, ln2_g, ln2_b):
    d = D_MODEL
    wi = w_in[l].astype(BF16)
    vec = lambda v: v.reshape(1, d).astype(F32)
    rows = lambda v: v.reshape(v.shape[:-1] + (SUBLANES, LANES)).astype(F32)
    wlru = jnp.concatenate([lru_wa[l], lru_wx[l]], axis=-1).astype(BF16)
    return dict(
        w_xr=wi[:, :d], w_grnn=wi[:, d:2 * d], w_conf=wi[:, 2 * d:4 * d], w_gate=wi[:, 4 * d:],
        rnn_cw=rows(rnn_conv_w[l]), rnn_cb=rows(rnn_conv_b[l]),
        wlru_f=wlru[0], wlru_b=wlru[1],
        ba_f=vec(lru_ba[l, 0]), ba_b=vec(lru_ba[l, 1]),
        bx_f=vec(lru_bx[l, 0]), bx_b=vec(lru_bx[l, 1]),
        lam_f=vec(lru_lambda[l, 0]), lam_b=vec(lru_lambda[l, 1]),
        w_rnn_proj=w_rnn_proj[l].astype(BF16),
        conf_cw=rows(conf_conv_w[l]), conf_cb=rows(conf_conv_b[l]),
        conf_ng=vec(conf_norm_g[l]), conf_nb=vec(conf_norm_b[l]),
        w_conf_proj=w_conf_proj[l].astype(BF16),
        gate_b=gate_b[l].astype(F32), w_out=w_out[l].astype(BF16),
        ln1_g=vec(ln1_g[l]), ln1_b=vec(ln1_b[l]),
        w_ffn_in=w_ffn_in[l].astype(BF16), w_ffn_out=w_ffn_out[l].astype(BF16),
        ln2_g=vec(ln2_g[l]), ln2_b=vec(ln2_b[l]),
    )


def _trunk(x, layers, alpha):
    for p in layers:
        xc, hb = _bwd_pass(x, p)
        x = _fwd_pass(x, xc, hb, p, alpha)
        x = _ffn_pass(x, p, alpha)
    return x


def kernel(x_prompt, x_sample, w_in, rnn_conv_w, rnn_conv_b, lru_wa, lru_ba, lru_wx, lru_bx, lru_lambda, w_rnn_proj, conf_conv_w, conf_conv_b, conf_norm_g, conf_norm_b, w_conf_proj, gate_b, w_out, ln1_g, ln1_b, w_ffn_in, w_ffn_out, ln2_g, ln2_b):
    weights = (w_in, rnn_conv_w, rnn_conv_b, lru_wa, lru_ba, lru_wx, lru_bx, lru_lambda,
               w_rnn_proj, conf_conv_w, conf_conv_b, conf_norm_g, conf_norm_b, w_conf_proj,
               gate_b, w_out, ln1_g, ln1_b, w_ffn_in, w_ffn_out, ln2_g, ln2_b)
    depth = w_in.shape[0]
    alpha = (2.0 * depth) ** 0.25
    layers = [_layer_params(l, *weights) for l in range(depth)]
    return (_trunk(x_prompt, layers, alpha), _trunk(x_sample, layers, alpha))
```

```python
import functools

import jax
import jax.numpy as jnp
from jax import lax
from jax.experimental import pallas as pl
from jax.experimental.pallas import tpu as pltpu

D_MODEL = 1024
N_HEADS = 8
HEAD = D_MODEL // N_HEADS
RNN_CONV_W = 4
CONF_CONV_W = 31
LRU_C = 8.0
LN_EPS = 1e-5
D_FF = 2816
FF_CHUNK = 256

SUBLANES = 8
LANES = 128
LANE_BLOCKS = D_MODEL // LANES

BF16_ROWS = 16
RNN_HALO = BF16_ROWS
CONF_HALO = BF16_ROWS
TIME_TILE = 256
FFN_ROWS = 512
VMEM_LIMIT_BYTES = 56 * 1024 * 1024

F32 = jnp.float32
BF16 = jnp.bfloat16


def _const_spec(shape):
    n = len(shape)
    return pl.BlockSpec(shape, lambda *_: (0,) * n, pipeline_mode=pl.Buffered(1))


def _to_rows(dst_ref, val):
    rows = val.shape[0]
    for c in range(LANE_BLOCKS):
        dst_ref[pl.ds(c, rows, stride=LANE_BLOCKS), :] = val[:, c * LANES:(c + 1) * LANES]


def _from_rows(src_ref, rows):
    cols = [src_ref[pl.ds(c, rows, stride=LANE_BLOCKS), :] for c in range(LANE_BLOCKS)]
    return jnp.concatenate(cols, axis=1)


def _row(ref, t):
    return ref[t * SUBLANES:(t + 1) * SUBLANES, :]


def _set_row(ref, t, val):
    ref[t * SUBLANES:(t + 1) * SUBLANES, :] = val


def _depthwise_conv_rows(src_ref, w_ref, b_ref, dst_ref, n_out, width, src_offset, block):
    bias = b_ref[...]
    for t0 in range(0, n_out, block):
        accs = [bias] * block
        for k in range(width):
            wk = w_ref[k]
            for tt in range(block):
                accs[tt] = accs[tt] + wk * _row(src_ref, t0 + tt + src_offset + k)
        for tt in range(block):
            _set_row(dst_ref, t0 + tt, accs[tt])


def _softplus(z):
    return jnp.maximum(z, 0.0) + jnp.log1p(jnp.exp(-jnp.abs(z)))


def _lru_gates(xc, wg_ref, ba_ref, bx_ref, lam_ref, a_rows, b_rows):
    rows = xc.shape[0]
    for h in range(N_HEADS):
        sl = slice(h * HEAD, (h + 1) * HEAD)
        xh = xc[:, sl]
        z = jnp.dot(xh.astype(BF16), wg_ref[h], preferred_element_type=F32)
        r = jax.nn.sigmoid(z[:, :HEAD] + ba_ref[:, sl])
        i = jax.nn.sigmoid(z[:, HEAD:] + bx_ref[:, sl])
        log_a = (-LRU_C * r) * _softplus(-lam_ref[:, sl])
        a = jnp.exp(log_a)
        b = jnp.sqrt(1.0 - a * a) * (i * xh)
        a_rows[pl.ds(h, rows, stride=LANE_BLOCKS), :] = a
        b_rows[pl.ds(h, rows, stride=LANE_BLOCKS), :] = b


def _layer_norm(x, g, b):
    mu = jnp.mean(x, axis=-1, keepdims=True)
    xm = x - mu
    var = jnp.mean(xm * xm, axis=-1, keepdims=True)
    return xm * lax.rsqrt(var + LN_EPS) * g + b


def _halo_tile(xs_ref, x_ref, prev_ref, next_ref, is_first, is_last):
    halo = prev_ref.shape[0]
    rows = x_ref.shape[0]
    xs_ref[0:halo, :] = jnp.where(is_first, 0.0, prev_ref[...]).astype(BF16)
    xs_ref[halo:halo + rows, :] = x_ref[...].astype(BF16)
    xs_ref[halo + rows:, :] = jnp.where(is_last, 0.0, next_ref[...]).astype(BF16)


def _bwd_pass_kernel(x_ref, xprev_ref, xnext_ref, wxr_ref, cw_ref, cb_ref, wg_ref, ba_ref, bx_ref,
                     lam_ref, xc_ref, hb_ref, xs, xr_rows, xc_rows, a_rows, b_rows, h_state):
    j = pl.program_id(1)
    n_t = pl.num_programs(1)
    jr = n_t - 1 - j
    rows = x_ref.shape[0]

    _halo_tile(xs, x_ref, xprev_ref, xnext_ref, jr == 0, jr == n_t - 1)
    xr = jnp.dot(xs[...], wxr_ref[...], preferred_element_type=F32)
    _to_rows(xr_rows, xr)
    pad_left = RNN_CONV_W // 2
    _depthwise_conv_rows(xr_rows, cw_ref, cb_ref, xc_rows, rows, RNN_CONV_W, RNN_HALO - pad_left, 8)
    xc = _from_rows(xc_rows, rows)
    xc_ref[...] = xc

    _lru_gates(xc, wg_ref, ba_ref, bx_ref, lam_ref, a_rows, b_rows)

    h = jnp.where(j == 0, 0.0, h_state[...])
    for t in range(rows - 1, -1, -1):
        h = _row(a_rows, t) * h + _row(b_rows, t)
        _set_row(hb_ref, t, h)
    h_state[...] = h


def _fwd_pass_kernel(x_ref, xprev_ref, xnext_ref, xc_ref, hb_ref,
                     wg_rnn_ref, wconf_ref, wgate_ref,
                     wlru_ref, ba_ref, bx_ref, lam_ref, wrp_ref,
                     ccw_ref, ccb_ref, cng_ref, cnb_ref, wcp_ref,
                     gb_ref, wout_ref, ln_g_ref, ln_b_ref,
                     out_ref, xs, u_rows, y_rows, a_rows, b_rows, h_rows, h_state, *, alpha):
    j = pl.program_id(1)
    n_t = pl.num_programs(1)
    rows = x_ref.shape[0]

    _halo_tile(xs, x_ref, xprev_ref, xnext_ref, j == 0, j == n_t - 1)
    xb = xs[CONF_HALO:CONF_HALO + rows, :]

    uv = jnp.dot(xs[...], wconf_ref[...], preferred_element_type=F32)
    u = uv[:, :D_MODEL] * jax.nn.sigmoid(uv[:, D_MODEL:])
    _to_rows(u_rows, u)
    pad_c = CONF_CONV_W // 2
    _depthwise_conv_rows(u_rows, ccw_ref, ccb_ref, y_rows, rows, CONF_CONV_W, CONF_HALO - pad_c, 8)
    y = _from_rows(y_rows, rows)
    un = jax.nn.silu(_layer_norm(y, cng_ref[...], cnb_ref[...]))
    y_b = jnp.dot(un.astype(BF16), wcp_ref[...], preferred_element_type=F32)

    xc = xc_ref[...]
    _lru_gates(xc, wlru_ref, ba_ref, bx_ref, lam_ref, a_rows, b_rows)
    h = jnp.where(j == 0, 0.0, h_state[...])
    for t in range(rows):
        h = _row(a_rows, t) * h + _row(b_rows, t)
        _set_row(h_rows, t, h + _row(hb_ref, t))
    h_state[...] = h
    h_sum = _from_rows(h_rows, rows)
    g_rnn = jnp.dot(xb, wg_rnn_ref[...], preferred_element_type=F32)
    y_a = jnp.dot((h_sum * jax.nn.gelu(g_rnn)).astype(BF16), wrp_ref[...], preferred_element_type=F32)

    gl = jnp.dot(xb, wgate_ref[...], preferred_element_type=F32)
    g0 = jax.nn.sigmoid(gl[:, :D_MODEL] + gb_ref[0:1, :])
    g1 = jax.nn.sigmoid(gl[:, D_MODEL:] + gb_ref[1:2, :])
    merged = g0 * y_a + g1 * y_b
    mo = jnp.dot(merged.astype(BF16), wout_ref[...], preferred_element_type=F32)
    out_ref[...] = _layer_norm(alpha * x_ref[...] + mo, ln_g_ref[...], ln_b_ref[...])


def _ffn_kernel(x_ref, win_ref, wout_ref, ln_g_ref, ln_b_ref, out_ref, *, alpha):
    x = x_ref[...]
    xb = x.astype(BF16)
    acc = jnp.zeros(x.shape, F32)
    for c in range(D_FF // FF_CHUNK):
        lo = c * FF_CHUNK
        g = jnp.dot(xb, win_ref[:, lo:lo + FF_CHUNK], preferred_element_type=F32)
        up = jnp.dot(xb, win_ref[:, D_FF + lo:D_FF + lo + FF_CHUNK], preferred_element_type=F32)
        hid = (jax.nn.silu(g) * up).astype(BF16)
        acc = acc + jnp.dot(hid, wout_ref[lo:lo + FF_CHUNK, :], preferred_element_type=F32)
    out_ref[...] = _layer_norm(alpha * x + acc, ln_g_ref[...], ln_b_ref[...])


def _compiler_params(n_grid):
    return pltpu.CompilerParams(dimension_semantics=("arbitrary",) * n_grid,
                                vmem_limit_bytes=VMEM_LIMIT_BYTES)


def _halo_specs(rows, halo, seq, time_index):
    per_tile = rows // halo
    last = seq // halo - 1
    main = pl.BlockSpec((None, rows, D_MODEL), lambda b, j: (b, time_index(j), 0))
    prev = pl.BlockSpec((None, halo, D_MODEL),
                        lambda b, j: (b, jnp.maximum(time_index(j) * per_tile - 1, 0), 0))
    nxt = pl.BlockSpec((None, halo, D_MODEL),
                       lambda b, j: (b, jnp.minimum((time_index(j) + 1) * per_tile, last), 0))
    return main, prev, nxt


def _bwd_pass(x, p):
    bsz, seq, _ = x.shape
    rows = TIME_TILE
    n_t = seq // rows
    rev = lambda j: n_t - 1 - j
    main, prev, nxt = _halo_specs(rows, RNN_HALO, seq, rev)
    vec = _const_spec((1, D_MODEL))
    return pl.pallas_call(
        _bwd_pass_kernel,
        grid=(bsz, n_t),
        in_specs=[main, prev, nxt,
                  _const_spec((D_MODEL, D_MODEL)),
                  _const_spec((RNN_CONV_W, SUBLANES, LANES)), _const_spec((SUBLANES, LANES)),
                  _const_spec((N_HEADS, HEAD, 2 * HEAD)), vec, vec, vec],
        out_specs=[pl.BlockSpec((None, rows, D_MODEL), lambda b, j: (b, rev(j), 0)),
                   pl.BlockSpec((None, rows * SUBLANES, LANES), lambda b, j: (b, rev(j), 0))],
        out_shape=[jax.ShapeDtypeStruct((bsz, seq, D_MODEL), F32),
                   jax.ShapeDtypeStruct((bsz, seq * SUBLANES, LANES), F32)],
        scratch_shapes=[pltpu.VMEM((rows + 2 * RNN_HALO, D_MODEL), BF16),
                        pltpu.VMEM(((rows + 2 * RNN_HALO) * SUBLANES, LANES), F32),
                        pltpu.VMEM((rows * SUBLANES, LANES), F32),
                        pltpu.VMEM((rows * SUBLANES, LANES), F32),
                        pltpu.VMEM((rows * SUBLANES, LANES), F32),
                        pltpu.VMEM((SUBLANES, LANES), F32)],
        compiler_params=_compiler_params(2),
        name="lru_bwd_pass",
    )(x, x, x, p["w_xr"], p["rnn_cw"], p["rnn_cb"], p["wlru_b"], p["ba_b"], p["bx_b"], p["lam_b"])


def _fwd_pass(x, xc, hb, p, alpha):
    bsz, seq, _ = x.shape
    rows = TIME_TILE
    n_t = seq // rows
    main, prev, nxt = _halo_specs(rows, CONF_HALO, seq, lambda j: j)
    tile = pl.BlockSpec((None, rows, D_MODEL), lambda b, j: (b, j, 0))
    vec = _const_spec((1, D_MODEL))
    sq = _const_spec((D_MODEL, D_MODEL))
    wide = _const_spec((D_MODEL, 2 * D_MODEL))
    return pl.pallas_call(
        functools.partial(_fwd_pass_kernel, alpha=alpha),
        grid=(bsz, n_t),
        in_specs=[main, prev, nxt, tile,
                  pl.BlockSpec((None, rows * SUBLANES, LANES), lambda b, j: (b, j, 0)),
                  sq, wide, wide,
                  _const_spec((N_HEADS, HEAD, 2 * HEAD)), vec, vec, vec, sq,
                  _const_spec((CONF_CONV_W, SUBLANES, LANES)), _const_spec((SUBLANES, LANES)),
                  vec, vec, sq,
                  _const_spec((2, D_MODEL)), sq, vec, vec],
        out_specs=tile,
        out_shape=jax.ShapeDtypeStruct((bsz, seq, D_MODEL), F32),
        scratch_shapes=[pltpu.VMEM((rows + 2 * CONF_HALO, D_MODEL), BF16),
                        pltpu.VMEM(((rows + 2 * CONF_HALO) * SUBLANES, LANES), F32),
                        pltpu.VMEM((rows * SUBLANES, LANES), F32),
                        pltpu.VMEM((rows * SUBLANES, LANES), F32),
                        pltpu.VMEM((rows * SUBLANES, LANES), F32),
                        pltpu.VMEM((rows * SUBLANES, LANES), F32),
                        pltpu.VMEM((SUBLANES, LANES), F32)],
        compiler_params=_compiler_params(2),
        name="mixer_fwd_pass",
    )(x, x, x, xc, hb,
      p["w_grnn"], p["w_conf"], p["w_gate"],
      p["wlru_f"], p["ba_f"], p["bx_f"], p["lam_f"], p["w_rnn_proj"],
      p["conf_cw"], p["conf_cb"], p["conf_ng"], p["conf_nb"], p["w_conf_proj"],
      p["gate_b"], p["w_out"], p["ln1_g"], p["ln1_b"])


def _ffn_pass(x, p, alpha):
    bsz, seq, _ = x.shape
    x2 = x.reshape(bsz * seq, D_MODEL)
    rows = FFN_ROWS
    tile = pl.BlockSpec((rows, D_MODEL), lambda i: (i, 0))
    vec = _const_spec((1, D_MODEL))
    out = pl.pallas_call(
        functools.partial(_ffn_kernel, alpha=alpha),
        grid=(bsz * seq // rows,),
        in_specs=[tile, _const_spec((D_MODEL, 2 * D_FF)), _const_spec((D_FF, D_MODEL)), vec, vec],
        out_specs=tile,
        out_shape=jax.ShapeDtypeStruct((bsz * seq, D_MODEL), F32),
        compiler_params=_compiler_params(1),
        name="ffn_pass",
    )(x2, p["w_ffn_in"], p["w_ffn_out"], p["ln2_g"], p["ln2_b"])
    return out.reshape(bsz, seq, D_MODEL)


def _layer_params(l, w_in, rnn_conv_w, rnn_conv_b, lru_wa, lru_ba, lru_wx, lru_bx, lru_lambda,
                  w_rnn_proj, conf_conv_w, conf_conv_b, conf_norm_g, conf_norm_b, w_conf_proj,
                  gate_b, w_out, ln1_g, ln1_b, w_ffn_in, w_ffn_out, ln2_g, ln2_b):
    d = D_MODEL
    wi = w_in[l].astype(BF16)
    vec = lambda v: v.reshape(1, d).astype(F32)
    rows = lambda v: v.reshape(v.shape[:-1] + (SUBLANES, LANES)).astype(F32)
    wlru = jnp.concatenate([lru_wa[l], lru_wx[l]], axis=-1).astype(BF16)
    return dict(
        w_xr=wi[:, :d], w_grnn=wi[:, d:2 * d], w_conf=wi[:, 2 * d:4 * d], w_gate=wi[:, 4 * d:],
        rnn_cw=rows(rnn_conv_w[l]), rnn_cb=rows(rnn_conv_b[l]),
        wlru_f=wlru[0], wlru_b=wlru[1],
        ba_f=vec(lru_ba[l, 0]), ba_b=vec(lru_ba[l, 1]),
        bx_f=vec(lru_bx[l, 0]), bx_b=vec(lru_bx[l, 1]),
        lam_f=vec(lru_lambda[l, 0]), lam_b=vec(lru_lambda[l, 1]),
        w_rnn_proj=w_rnn_proj[l].astype(BF16),
        conf_cw=rows(conf_conv_w[l]), conf_cb=rows(conf_conv_b[l]),
        conf_ng=vec(conf_norm_g[l]), conf_nb=vec(conf_norm_b[l]),
        w_conf_proj=w_conf_proj[l].astype(BF16),
        gate_b=gate_b[l].astype(F32), w_out=w_out[l].astype(BF16),
        ln1_g=vec(ln1_g[l]), ln1_b=vec(ln1_b[l]),
        w_ffn_in=w_ffn_in[l].astype(BF16), w_ffn_out=w_ffn_out[l].astype(BF16),
        ln2_g=vec(ln2_g[l]), ln2_b=vec(ln2_b[l]),
    )


def _trunk(x, layers, alpha):
    for p in layers:
        xc, hb = _bwd_pass(x, p)
        x = _fwd_pass(x, xc, hb, p, alpha)
        x = _ffn_pass(x, p, alpha)
    return x


def kernel(x_prompt, x_sample, w_in, rnn_conv_w, rnn_conv_b, lru_wa, lru_ba, lru_wx, lru_bx, lru_lambda, w_rnn_proj, conf_conv_w, conf_conv_b, conf_norm_g, conf_norm_b, w_conf_proj, gate_b, w_out, ln1_g, ln1_b, w_ffn_in, w_ffn_out, ln2_g, ln2_b):
    weights = (w_in, rnn_conv_w, rnn_conv_b, lru_wa, lru_ba, lru_wx, lru_bx, lru_lambda,
               w_rnn_proj, conf_conv_w, conf_conv_b, conf_norm_g, conf_norm_b, w_conf_proj,
               gate_b, w_out, ln1_g, ln1_b, w_ffn_in, w_ffn_out, ln2_g, ln2_b)
    depth = w_in.shape[0]
    alpha = (2.0 * depth) ** 0.25
    layers = [_layer_params(l, *weights) for l in range(depth)]
    return (_trunk(x_prompt, layers, alpha), _trunk(x_sample, layers, alpha))
```

```python
import functools

import jax
import jax.numpy as jnp
import numpy as np
from jax import lax
from jax.experimental import pallas as pl
from jax.experimental.pallas import tpu as pltpu

D_MODEL = 1024
N_HEADS = 8
HEAD = D_MODEL // N_HEADS
RNN_CONV_W = 4
CONF_CONV_W = 31
LRU_C = 8.0
LN_EPS = 1e-5
D_FF = 2816
FF_CHUNK = 256

SUBLANES = 8
LANES = 128
LANE_BLOCKS = D_MODEL // LANES

HALO = 16
BWD_TIME_TILE = 256
FWD_TIME_TILE = 512
FFN_ROWS = 512
VMEM_LIMIT_BYTES = 56 * 1024 * 1024
F32_TINY = 1e-30

DFT_OUT = 128
DFT_LEN = DFT_OUT + 2 * HALO
DFT_BINS = DFT_LEN // 2 + 1
DFT_HALF = 88
TAPS_PAD = 128

F32 = jnp.float32
BF16 = jnp.bfloat16


def _const_spec(shape):
    n = len(shape)
    return pl.BlockSpec(shape, lambda *_: (0,) * n, pipeline_mode=pl.Buffered(1))


def _to_rows(dst_ref, val):
    rows = val.shape[0]
    for c in range(LANE_BLOCKS):
        dst_ref[pl.ds(c, rows, stride=LANE_BLOCKS), :] = val[:, c * LANES:(c + 1) * LANES]


def _from_rows(src_ref, rows):
    cols = [src_ref[pl.ds(c, rows, stride=LANE_BLOCKS), :] for c in range(LANE_BLOCKS)]
    return jnp.concatenate(cols, axis=1)


def _row(ref, t):
    return ref[t * SUBLANES:(t + 1) * SUBLANES, :]


def _set_row(ref, t, val):
    ref[t * SUBLANES:(t + 1) * SUBLANES, :] = val


def _depthwise_conv_rows(src_ref, w_ref, b_ref, dst_ref, n_out, width, src_offset, block):
    bias = b_ref[...]
    for t0 in range(0, n_out, block):
        accs = [bias] * block
        for k in range(width):
            wk = w_ref[k]
            for tt in range(block):
                accs[tt] = accs[tt] + wk * _row(src_ref, t0 + tt + src_offset + k)
        for tt in range(block):
            _set_row(dst_ref, t0 + tt, accs[tt])


def _pack_rows(w):
    bits = lax.bitcast_convert_type(w.astype(BF16), jnp.uint16).astype(jnp.uint32)
    return bits[..., 0::2, :] | (bits[..., 1::2, :] << 16)


def _mm(lhs, w_words):
    return jnp.dot(lhs, pltpu.bitcast(w_words, BF16), preferred_element_type=F32)


def _dft_matrices():
    n = np.arange(DFT_LEN)
    f = np.arange(DFT_BINS)
    ang = 2.0 * np.pi * np.outer(f, n) / DFT_LEN
    fwd = np.zeros((2 * DFT_HALF, DFT_LEN))
    fwd[:DFT_BINS] = np.cos(ang)
    fwd[DFT_HALF:DFT_HALF + DFT_BINS] = -np.sin(ang)
    i = np.arange(DFT_OUT)
    ang_i = 2.0 * np.pi * np.outer(i, f) / DFT_LEN
    c = np.full(DFT_BINS, 2.0)
    c[0] = c[-1] = 1.0
    inv = np.zeros((DFT_OUT, 2 * DFT_HALF))
    inv[:, :DFT_BINS] = c * np.cos(ang_i) / DFT_LEN
    inv[:, DFT_HALF:DFT_HALF + DFT_BINS] = -c * np.sin(ang_i) / DFT_LEN
    k = np.arange(CONF_CONV_W)
    ang_k = 2.0 * np.pi * np.outer(f, 1 + k) / DFT_LEN
    taps = np.zeros((2 * DFT_HALF, TAPS_PAD))
    taps[:DFT_BINS, :CONF_CONV_W] = np.cos(ang_k)
    taps[DFT_HALF:DFT_HALF + DFT_BINS, :CONF_CONV_W] = np.sin(ang_k)
    return jnp.asarray(fwd, BF16), jnp.asarray(inv, BF16), jnp.asarray(taps, F32)


def _softplus(z):
    return jnp.maximum(z, 0.0) + jnp.log1p(jnp.exp(-jnp.abs(z)))


def _lru_gates(xc, wg_ref, ba_ref, bx_ref, lam_ref, a_rows, b_rows):
    rows = xc.shape[0]
    for h in range(N_HEADS):
        sl = slice(h * HEAD, (h + 1) * HEAD)
        xh = xc[:, sl]
        z = _mm(xh.astype(BF16), wg_ref[h])
        r = jax.nn.sigmoid(z[:, :HEAD] + ba_ref[:, sl])
        i = jax.nn.sigmoid(z[:, HEAD:] + bx_ref[:, sl])
        log_a = (-LRU_C * r) * _softplus(-lam_ref[:, sl])
        a = jnp.exp(log_a)
        v = 1.0 - a * a
        b = (v * lax.rsqrt(jnp.maximum(v, F32_TINY))) * (i * xh)
        a_rows[pl.ds(h, rows, stride=LANE_BLOCKS), :] = a
        b_rows[pl.ds(h, rows, stride=LANE_BLOCKS), :] = b


def _layer_norm(x, g, b):
    mu = jnp.mean(x, axis=-1, keepdims=True)
    xm = x - mu
    var = jnp.mean(xm * xm, axis=-1, keepdims=True)
    return xm * lax.rsqrt(var + LN_EPS) * g + b


def _halo_tile(xs_ref, x_ref, prev_ref, next_ref, is_first, is_last):
    rows = x_ref.shape[0]
    xs_ref[0:HALO, :] = jnp.where(is_first, 0.0, prev_ref[...]).astype(BF16)
    xs_ref[HALO:HALO + rows, :] = x_ref[...].astype(BF16)
    xs_ref[HALO + rows:, :] = jnp.where(is_last, 0.0, next_ref[...]).astype(BF16)


def _bwd_pass_kernel(x_ref, xprev_ref, xnext_ref, wxr_ref, cw_ref, cb_ref, wg_ref, ba_ref, bx_ref,
                     lam_ref, xc_ref, hb_ref, xs, xr_rows, xc_rows, a_rows, b_rows, h_state):
    j = pl.program_id(1)
    n_t = pl.num_programs(1)
    jr = n_t - 1 - j
    rows = x_ref.shape[0]

    _halo_tile(xs, x_ref, xprev_ref, xnext_ref, jr == 0, jr == n_t - 1)
    _to_rows(xr_rows, _mm(xs[...], wxr_ref[...]))
    pad_left = RNN_CONV_W // 2
    _depthwise_conv_rows(xr_rows, cw_ref, cb_ref, xc_rows, rows, RNN_CONV_W, HALO - pad_left, 8)
    xc = _from_rows(xc_rows, rows)
    xc_ref[...] = xc

    _lru_gates(xc, wg_ref, ba_ref, bx_ref, lam_ref, a_rows, b_rows)

    h = jnp.where(j == 0, 0.0, h_state[...])
    for t in range(rows - 1, -1, -1):
        h = _row(a_rows, t) * h + _row(b_rows, t)
        _set_row(hb_ref, t, h)
    h_state[...] = h


def _fwd_pass_kernel(x_ref, xprev_ref, xnext_ref, xc_ref, hb_ref,
                     wg_rnn_ref, wconf_ref, wgate_ref,
                     wlru_ref, ba_ref, bx_ref, lam_ref, wrp_ref,
                     ccw_ref, ccb_ref, dft_fwd_ref, dft_inv_ref, dft_taps_ref, cng_ref, cnb_ref, wcp_ref,
                     gb_ref, wout_ref, ln_g_ref, ln_b_ref,
                     out_ref, xs, u_s, w_spec, a_rows, b_rows, h_rows, h_state, *, alpha):
    j = pl.program_id(1)
    n_t = pl.num_programs(1)
    rows = x_ref.shape[0]

    @pl.when(j == 0)
    def _():
        w_spec[...] = jnp.dot(dft_taps_ref[...], ccw_ref[...], preferred_element_type=F32,
                              precision=lax.Precision.HIGHEST)

    _halo_tile(xs, x_ref, xprev_ref, xnext_ref, j == 0, j == n_t - 1)
    xb = xs[HALO:HALO + rows, :]

    for c in range(LANE_BLOCKS):
        uv = _mm(xs[...], wconf_ref[:, 2 * LANES * c:2 * LANES * (c + 1)])
        u_s[:, c * LANES:(c + 1) * LANES] = uv[:, :LANES] * jax.nn.sigmoid(uv[:, LANES:])

    g_re, g_im = w_spec[0:DFT_HALF, :], w_spec[DFT_HALF:, :]
    y_blocks = []
    for blk in range(rows // DFT_OUT):
        ub = u_s[blk * DFT_OUT:blk * DFT_OUT + DFT_LEN, :].astype(BF16)
        spec = jnp.dot(dft_fwd_ref[...], ub, preferred_element_type=F32)
        s_re, s_im = spec[0:DFT_HALF], spec[DFT_HALF:]
        prod = jnp.concatenate([s_re * g_re - s_im * g_im, s_re * g_im + s_im * g_re], axis=0)
        y_blocks.append(jnp.dot(dft_inv_ref[...], prod.astype(BF16), preferred_element_type=F32))
    y = jnp.concatenate(y_blocks, axis=0) + ccb_ref[...]
    un = jax.nn.silu(_layer_norm(y, cng_ref[...], cnb_ref[...]))
    y_b = _mm(un.astype(BF16), wcp_ref[...])

    xc = xc_ref[...]
    _lru_gates(xc, wlru_ref, ba_ref, bx_ref, lam_ref, a_rows, b_rows)
    h = jnp.where(j == 0, 0.0, h_state[...])
    for t in range(rows):
        h = _row(a_rows, t) * h + _row(b_rows, t)
        _set_row(h_rows, t, h + _row(hb_ref, t))
    h_state[...] = h
    h_sum = _from_rows(h_rows, rows)
    g_rnn = _mm(xb, wg_rnn_ref[...])
    y_a = _mm((h_sum * jax.nn.gelu(g_rnn)).astype(BF16), wrp_ref[...])

    gl = _mm(xb, wgate_ref[...])
    g0 = jax.nn.sigmoid(gl[:, :D_MODEL] + gb_ref[0:1, :])
    g1 = jax.nn.sigmoid(gl[:, D_MODEL:] + gb_ref[1:2, :])
    merged = g0 * y_a + g1 * y_b
    mo = _mm(merged.astype(BF16), wout_ref[...])
    out_ref[...] = _layer_norm(alpha * x_ref[...] + mo, ln_g_ref[...], ln_b_ref[...])


def _ffn_kernel(x_ref, win_ref, wout_ref, ln_g_ref, ln_b_ref, out_ref, *, alpha):
    x = x_ref[...]
    xb = x.astype(BF16)
    acc = jnp.zeros(x.shape, F32)
    for c in range(D_FF // FF_CHUNK):
        lo = c * FF_CHUNK
        g = _mm(xb, win_ref[:, lo:lo + FF_CHUNK])
        up = _mm(xb, win_ref[:, D_FF + lo:D_FF + lo + FF_CHUNK])
        hid = (jax.nn.silu(g) * up).astype(BF16)
        acc = acc + _mm(hid, wout_ref[lo // 2:(lo + FF_CHUNK) // 2, :])
    out_ref[...] = _layer_norm(alpha * x + acc, ln_g_ref[...], ln_b_ref[...])


def _compiler_params(n_grid):
    return pltpu.CompilerParams(dimension_semantics=("arbitrary",) * n_grid,
                                vmem_limit_bytes=VMEM_LIMIT_BYTES)


def _rows_scratch(rows):
    return pltpu.VMEM((rows * SUBLANES, LANES), F32)


def _halo_specs(rows, seq, time_index):
    per_tile = rows // HALO
    last = seq // HALO - 1
    main = pl.BlockSpec((None, rows, D_MODEL), lambda b, j: (b, time_index(j), 0))
    prev = pl.BlockSpec((None, HALO, D_MODEL),
                        lambda b, j: (b, jnp.maximum(time_index(j) * per_tile - 1, 0), 0))
    nxt = pl.BlockSpec((None, HALO, D_MODEL),
                       lambda b, j: (b, jnp.minimum((time_index(j) + 1) * per_tile, last), 0))
    return main, prev, nxt


def _bwd_pass(x, p):
    bsz, seq, _ = x.shape
    rows = BWD_TIME_TILE
    n_t = seq // rows
    rev = lambda j: n_t - 1 - j
    main, prev, nxt = _halo_specs(rows, seq, rev)
    vec = _const_spec((1, D_MODEL))
    return pl.pallas_call(
        _bwd_pass_kernel,
        grid=(bsz, n_t),
        in_specs=[main, prev, nxt,
                  _const_spec((D_MODEL // 2, D_MODEL)),
                  _const_spec((RNN_CONV_W, SUBLANES, LANES)), _const_spec((SUBLANES, LANES)),
                  _const_spec((N_HEADS, HEAD // 2, 2 * HEAD)), vec, vec, vec],
        out_specs=[pl.BlockSpec((None, rows, D_MODEL), lambda b, j: (b, rev(j), 0)),
                   pl.BlockSpec((None, rows * SUBLANES, LANES), lambda b, j: (b, rev(j), 0))],
        out_shape=[jax.ShapeDtypeStruct((bsz, seq, D_MODEL), F32),
                   jax.ShapeDtypeStruct((bsz, seq * SUBLANES, LANES), F32)],
        scratch_shapes=[pltpu.VMEM((rows + 2 * HALO, D_MODEL), BF16),
                        _rows_scratch(rows + 2 * HALO), _rows_scratch(rows),
                        _rows_scratch(rows), _rows_scratch(rows), _rows_scratch(1)],
        compiler_params=_compiler_params(2),
        name="lru_bwd_pass",
    )(x, x, x, p["w_xr"], p["rnn_cw"], p["rnn_cb"], p["wlru_b"], p["ba_b"], p["bx_b"], p["lam_b"])


def _fwd_pass(x, xc, hb, p, alpha):
    bsz, seq, _ = x.shape
    rows = FWD_TIME_TILE
    n_t = seq // rows
    main, prev, nxt = _halo_specs(rows, seq, lambda j: j)
    tile = pl.BlockSpec((None, rows, D_MODEL), lambda b, j: (b, j, 0))
    vec = _const_spec((1, D_MODEL))
    sq = _const_spec((D_MODEL // 2, D_MODEL))
    wide = _const_spec((D_MODEL // 2, 2 * D_MODEL))
    return pl.pallas_call(
        functools.partial(_fwd_pass_kernel, alpha=alpha),
        grid=(bsz, n_t),
        in_specs=[main, prev, nxt, tile,
                  pl.BlockSpec((None, rows * SUBLANES, LANES), lambda b, j: (b, j, 0)),
                  sq, wide, wide,
                  _const_spec((N_HEADS, HEAD // 2, 2 * HEAD)), vec, vec, vec, sq,
                  _const_spec((TAPS_PAD, D_MODEL)), vec,
                  _const_spec((2 * DFT_HALF, DFT_LEN)), _const_spec((DFT_OUT, 2 * DFT_HALF)),
                  _const_spec((2 * DFT_HALF, TAPS_PAD)),
                  vec, vec, sq,
                  _const_spec((2, D_MODEL)), sq, vec, vec],
        out_specs=tile,
        out_shape=jax.ShapeDtypeStruct((bsz, seq, D_MODEL), F32),
        scratch_shapes=[pltpu.VMEM((rows + 2 * HALO, D_MODEL), BF16),
                        pltpu.VMEM((rows + 2 * HALO, D_MODEL), F32),
                        pltpu.VMEM((2 * DFT_HALF, D_MODEL), F32),
                        _rows_scratch(rows), _rows_scratch(rows), _rows_scratch(rows),
                        _rows_scratch(1)],
        compiler_params=_compiler_params(2),
        name="mixer_fwd_pass",
    )(x, x, x, xc, hb,
      p["w_grnn"], p["w_conf"], p["w_gate"],
      p["wlru_f"], p["ba_f"], p["bx_f"], p["lam_f"], p["w_rnn_proj"],
      p["conf_cw"], p["conf_cb"], p["dft_fwd"], p["dft_inv"], p["dft_taps"],
      p["conf_ng"], p["conf_nb"], p["w_conf_proj"],
      p["gate_b"], p["w_out"], p["ln1_g"], p["ln1_b"])


def _ffn_pass(x, p, alpha):
    bsz, seq, _ = x.shape
    x2 = x.reshape(bsz * seq, D_MODEL)
    rows = FFN_ROWS
    tile = pl.BlockSpec((rows, D_MODEL), lambda i: (i, 0))
    vec = _const_spec((1, D_MODEL))
    out = pl.pallas_call(
        functools.partial(_ffn_kernel, alpha=alpha),
        grid=(bsz * seq // rows,),
        in_specs=[tile, _const_spec((D_MODEL // 2, 2 * D_FF)), _const_spec((D_FF // 2, D_MODEL)),
                  vec, vec],
        out_specs=tile,
        out_shape=jax.ShapeDtypeStruct((bsz * seq, D_MODEL), F32),
        compiler_params=_compiler_params(1),
        name="ffn_pass",
    )(x2, p["w_ffn_in"], p["w_ffn_out"], p["ln2_g"], p["ln2_b"])
    return out.reshape(bsz, seq, D_MODEL)


def _layer_params(l, w_in, rnn_conv_w, rnn_conv_b, lru_wa, lru_ba, lru_wx, lru_bx, lru_lambda,
                  w_rnn_proj, conf_conv_w, conf_conv_b, conf_norm_g, conf_norm_b, w_conf_proj,
                  gate_b, w_out, ln1_g, ln1_b, w_ffn_in, w_ffn_out, ln2_g, ln2_b):
    d = D_MODEL
    wi = _pack_rows(w_in[l])
    vec = lambda v: v.reshape(1, d).astype(F32)
    rows = lambda v: v.reshape(v.shape[:-1] + (SUBLANES, LANES)).astype(F32)
    wlru = _pack_rows(jnp.concatenate([lru_wa[l], lru_wx[l]], axis=-1))
    w_conf = wi[:, 2 * d:4 * d].reshape(d // 2, 2, LANE_BLOCKS, LANES)
    w_conf = jnp.swapaxes(w_conf, 1, 2).reshape(d // 2, 2 * d)
    dft_fwd, dft_inv, dft_taps = _dft_matrices()
    return dict(
        w_xr=wi[:, :d], w_grnn=wi[:, d:2 * d], w_conf=w_conf, w_gate=wi[:, 4 * d:],
        rnn_cw=rows(rnn_conv_w[l]), rnn_cb=rows(rnn_conv_b[l]),
        wlru_f=wlru[0], wlru_b=wlru[1],
        ba_f=vec(lru_ba[l, 0]), ba_b=vec(lru_ba[l, 1]),
        bx_f=vec(lru_bx[l, 0]), bx_b=vec(lru_bx[l, 1]),
        lam_f=vec(lru_lambda[l, 0]), lam_b=vec(lru_lambda[l, 1]),
        w_rnn_proj=_pack_rows(w_rnn_proj[l]),
        conf_cw=jnp.pad(conf_conv_w[l].astype(F32), ((0, TAPS_PAD - CONF_CONV_W), (0, 0))),
        conf_cb=vec(conf_conv_b[l]),
        dft_fwd=dft_fwd, dft_inv=dft_inv, dft_taps=dft_taps,
        conf_ng=vec(conf_norm_g[l]), conf_nb=vec(conf_norm_b[l]),
        w_conf_proj=_pack_rows(w_conf_proj[l]),
        gate_b=gate_b[l].astype(F32), w_out=_pack_rows(w_out[l]),
        ln1_g=vec(ln1_g[l]), ln1_b=vec(ln1_b[l]),
        w_ffn_in=_pack_rows(w_ffn_in[l]), w_ffn_out=_pack_rows(w_ffn_out[l]),
        ln2_g=vec(ln2_g[l]), ln2_b=vec(ln2_b[l]),
    )


def _trunk(x, layers, alpha):
    for p in layers:
        xc, hb = _bwd_pass(x, p)
        x = _fwd_pass(x, xc, hb, p, alpha)
        x = _ffn_pass(x, p, alpha)
    return x


def kernel(x_prompt, x_sample, w_in, rnn_conv_w, rnn_conv_b, lru_wa, lru_ba, lru_wx, lru_bx, lru_lambda, w_rnn_proj, conf_conv_w, conf_conv_b, conf_norm_g, conf_norm_b, w_conf_proj, gate_b, w_out, ln1_g, ln1_b, w_ffn_in, w_ffn_out, ln2_g, ln2_b):
    weights = (w_in, rnn_conv_w, rnn_conv_b, lru_wa, lru_ba, lru_wx, lru_bx, lru_lambda,
               w_rnn_proj, conf_conv_w, conf_conv_b, conf_norm_g, conf_norm_b, w_conf_proj,
               gate_b, w_out, ln1_g, ln1_b, w_ffn_in, w_ffn_out, ln2_g, ln2_b)
    depth = w_in.shape[0]
    alpha = (2.0 * depth) ** 0.25
    layers = [_layer_params(l, *weights) for l in range(depth)]
    return (_trunk(x_prompt, layers, alpha), _trunk(x_sample, layers, alpha))
```

```python
import functools

import jax
import jax.numpy as jnp
import numpy as np
from jax import lax
from jax.experimental import pallas as pl
from jax.experimental.pallas import tpu as pltpu

D_MODEL = 1024
N_HEADS = 8
HEAD = D_MODEL // N_HEADS
RNN_CONV_W = 4
CONF_CONV_W = 31
LRU_C = 8.0
LN_EPS = 1e-5
D_FF = 2816
FF_CHUNK = 256

SUBLANES = 8
LANES = 128
LANE_BLOCKS = D_MODEL // LANES

HALO = 16
BWD_TIME_TILE = 512
FWD_TIME_TILE = 512
FFN_ROWS = 512
VMEM_LIMIT_BYTES = 56 * 1024 * 1024
F32_TINY = 1e-30
BWD_SCAN_GROUP = 8
FWD_SCAN_GROUP = 1

DFT_OUT = 128
DFT_LEN = DFT_OUT + 2 * HALO
DFT_BINS = DFT_LEN // 2 + 1
DFT_HALF = 88
TAPS_PAD = 128

F32 = jnp.float32
BF16 = jnp.bfloat16


def _const_spec(shape):
    n = len(shape)
    return pl.BlockSpec(shape, lambda *_: (0,) * n, pipeline_mode=pl.Buffered(1))


def _to_rows(dst_ref, val):
    rows = val.shape[0]
    for c in range(LANE_BLOCKS):
        dst_ref[pl.ds(c, rows, stride=LANE_BLOCKS), :] = val[:, c * LANES:(c + 1) * LANES]


def _from_rows(src_ref, rows):
    cols = [src_ref[pl.ds(c, rows, stride=LANE_BLOCKS), :] for c in range(LANE_BLOCKS)]
    return jnp.concatenate(cols, axis=1)


def _row(ref, t):
    return ref[t * SUBLANES:(t + 1) * SUBLANES, :]


def _set_row(ref, t, val):
    ref[t * SUBLANES:(t + 1) * SUBLANES, :] = val


def _depthwise_conv_rows(src_ref, w_ref, b_ref, dst_ref, n_out, width, src_offset, block):
    bias = b_ref[...]
    for t0 in range(0, n_out, block):
        accs = [bias] * block
        for k in range(width):
            wk = w_ref[k]
            for tt in range(block):
                accs[tt] = accs[tt] + wk * _row(src_ref, t0 + tt + src_offset + k)
        for tt in range(block):
            _set_row(dst_ref, t0 + tt, accs[tt])


def _pack_rows(w):
    k, n = w.shape[-2:]
    pairs = w.reshape(w.shape[:-2] + (k // 2, 2 * n))
    bits = lax.bitcast_convert_type(pairs.astype(BF16), jnp.uint16).astype(jnp.uint32)
    return bits[..., :n] | (bits[..., n:] << 16)


def _mm(lhs, w_words):
    return jnp.dot(lhs, pltpu.bitcast(w_words, BF16), preferred_element_type=F32)


def _dft_matrices():
    n = np.arange(DFT_LEN)
    f = np.arange(DFT_BINS)
    ang = 2.0 * np.pi * np.outer(f, n) / DFT_LEN
    fwd = np.zeros((2 * DFT_HALF, DFT_LEN))
    fwd[:DFT_BINS] = np.cos(ang)
    fwd[DFT_HALF:DFT_HALF + DFT_BINS] = -np.sin(ang)
    i = np.arange(DFT_OUT)
    ang_i = 2.0 * np.pi * np.outer(i, f) / DFT_LEN
    c = np.full(DFT_BINS, 2.0)
    c[0] = c[-1] = 1.0
    inv = np.zeros((DFT_OUT, 2 * DFT_HALF))
    inv[:, :DFT_BINS] = c * np.cos(ang_i) / DFT_LEN
    inv[:, DFT_HALF:DFT_HALF + DFT_BINS] = -c * np.sin(ang_i) / DFT_LEN
    k = np.arange(CONF_CONV_W)
    ang_k = 2.0 * np.pi * np.outer(f, 1 + k) / DFT_LEN
    taps = np.zeros((2 * DFT_HALF, TAPS_PAD))
    taps[:DFT_BINS, :CONF_CONV_W] = np.cos(ang_k)
    taps[DFT_HALF:DFT_HALF + DFT_BINS, :CONF_CONV_W] = np.sin(ang_k)
    return jnp.asarray(fwd, BF16), jnp.asarray(inv, BF16), jnp.asarray(taps, F32)


def _softplus(z):
    return jnp.maximum(z, 0.0) + jnp.log1p(jnp.exp(-jnp.abs(z)))


def _lru_gates(xc, wg_ref, ba_ref, bx_ref, lam_ref, a_rows, b_rows):
    rows = xc.shape[0]
    for h in range(N_HEADS):
        sl = slice(h * HEAD, (h + 1) * HEAD)
        xh = xc[:, sl]
        z = _mm(xh.astype(BF16), wg_ref[h])
        r = jax.nn.sigmoid(z[:, :HEAD] + ba_ref[:, sl])
        i = jax.nn.sigmoid(z[:, HEAD:] + bx_ref[:, sl])
        log_a = (-LRU_C * r) * _softplus(-lam_ref[:, sl])
        a = jnp.exp(log_a)
        v = 1.0 - a * a
        b = (v * lax.rsqrt(jnp.maximum(v, F32_TINY))) * (i * xh)
        a_rows[pl.ds(h, rows, stride=LANE_BLOCKS), :] = a
        b_rows[pl.ds(h, rows, stride=LANE_BLOCKS), :] = b


def _scan_rows(a_rows, b_rows, h, order, group, emit):
    for g0 in range(0, len(order), group):
        coef_a = coef_b = None
        for t in order[g0:g0 + group]:
            a, b = _row(a_rows, t), _row(b_rows, t)
            if coef_a is None:
                coef_a, coef_b = a, b
            else:
                coef_a, coef_b = a * coef_a, a * coef_b + b
            h_t = coef_a * h + coef_b
            emit(t, h_t)
        h = h_t
    return h


def _layer_norm(x, g, b):
    mu = jnp.mean(x, axis=-1, keepdims=True)
    xm = x - mu
    var = jnp.mean(xm * xm, axis=-1, keepdims=True)
    return xm * lax.rsqrt(var + LN_EPS) * g + b


def _halo_tile(xs_ref, x_ref, prev_ref, next_ref, is_first, is_last):
    rows = x_ref.shape[0]
    xs_ref[0:HALO, :] = jnp.where(is_first, 0.0, prev_ref[...]).astype(BF16)
    xs_ref[HALO:HALO + rows, :] = x_ref[...].astype(BF16)
    xs_ref[HALO + rows:, :] = jnp.where(is_last, 0.0, next_ref[...]).astype(BF16)


def _bwd_pass_kernel(x_ref, xprev_ref, xnext_ref, wxr_ref, cw_ref, cb_ref, wg_ref, ba_ref, bx_ref,
                     lam_ref, xc_ref, hb_ref, xs, xr_rows, xc_rows, a_rows, b_rows, h_state):
    j = pl.program_id(1)
    n_t = pl.num_programs(1)
    jr = n_t - 1 - j
    rows = x_ref.shape[0]

    _halo_tile(xs, x_ref, xprev_ref, xnext_ref, jr == 0, jr == n_t - 1)
    _to_rows(xr_rows, _mm(xs[...], wxr_ref[...]))
    pad_left = RNN_CONV_W // 2
    _depthwise_conv_rows(xr_rows, cw_ref, cb_ref, xc_rows, rows, RNN_CONV_W, HALO - pad_left, 8)
    xc = _from_rows(xc_rows, rows)
    xc_ref[...] = xc

    _lru_gates(xc, wg_ref, ba_ref, bx_ref, lam_ref, a_rows, b_rows)

    h0 = jnp.where(j == 0, 0.0, h_state[...])
    h_state[...] = _scan_rows(a_rows, b_rows, h0, list(range(rows - 1, -1, -1)), BWD_SCAN_GROUP,
                              lambda t, h: _set_row(hb_ref, t, h))


def _fwd_pass_kernel(x_ref, xprev_ref, xnext_ref, xc_ref, hb_ref,
                     wg_rnn_ref, wconf_ref, wgate_ref,
                     wlru_ref, ba_ref, bx_ref, lam_ref, wrp_ref,
                     ccw_ref, ccb_ref, dft_fwd_ref, dft_inv_ref, dft_taps_ref, cng_ref, cnb_ref, wcp_ref,
                     gb_ref, wout_ref, ln_g_ref, ln_b_ref,
                     out_ref, xs, u_s, w_spec, a_rows, b_rows, h_rows, h_state, *, alpha):
    j = pl.program_id(1)
    n_t = pl.num_programs(1)
    rows = x_ref.shape[0]

    @pl.when(j == 0)
    def _():
        w_spec[...] = jnp.dot(dft_taps_ref[...], ccw_ref[...], preferred_element_type=F32,
                              precision=lax.Precision.HIGHEST)

    _halo_tile(xs, x_ref, xprev_ref, xnext_ref, j == 0, j == n_t - 1)
    xb = xs[HALO:HALO + rows, :]

    for c in range(LANE_BLOCKS):
        uv = _mm(xs[...], wconf_ref[:, 2 * LANES * c:2 * LANES * (c + 1)])
        u_s[:, c * LANES:(c + 1) * LANES] = uv[:, :LANES] * jax.nn.sigmoid(uv[:, LANES:])

    g_re, g_im = w_spec[0:DFT_HALF, :], w_spec[DFT_HALF:, :]
    y_blocks = []
    for blk in range(rows // DFT_OUT):
        ub = u_s[blk * DFT_OUT:blk * DFT_OUT + DFT_LEN, :].astype(BF16)
        spec = jnp.dot(dft_fwd_ref[...], ub, preferred_element_type=F32)
        s_re, s_im = spec[0:DFT_HALF], spec[DFT_HALF:]
        prod = jnp.concatenate([s_re * g_re - s_im * g_im, s_re * g_im + s_im * g_re], axis=0)
        y_blocks.append(jnp.dot(dft_inv_ref[...], prod.astype(BF16), preferred_element_type=F32))
    y = jnp.concatenate(y_blocks, axis=0) + ccb_ref[...]
    un = jax.nn.silu(_layer_norm(y, cng_ref[...], cnb_ref[...]))
    y_b = _mm(un.astype(BF16), wcp_ref[...])

    xc = xc_ref[...]
    _lru_gates(xc, wlru_ref, ba_ref, bx_ref, lam_ref, a_rows, b_rows)
    h0 = jnp.where(j == 0, 0.0, h_state[...])
    h_state[...] = _scan_rows(a_rows, b_rows, h0, list(range(rows)), FWD_SCAN_GROUP,
                              lambda t, h: _set_row(h_rows, t, h + _row(hb_ref, t)))
    h_sum = _from_rows(h_rows, rows)
    g_rnn = _mm(xb, wg_rnn_ref[...])
    y_a = _mm((h_sum * jax.nn.gelu(g_rnn)).astype(BF16), wrp_ref[...])

    gl = _mm(xb, wgate_ref[...])
    g0 = jax.nn.sigmoid(gl[:, :D_MODEL] + gb_ref[0:1, :])
    g1 = jax.nn.sigmoid(gl[:, D_MODEL:] + gb_ref[1:2, :])
    merged = g0 * y_a + g1 * y_b
    mo = _mm(merged.astype(BF16), wout_ref[...])
    out_ref[...] = _layer_norm(alpha * x_ref[...] + mo, ln_g_ref[...], ln_b_ref[...])


def _ffn_kernel(x_ref, win_ref, wout_ref, ln_g_ref, ln_b_ref, out_ref, *, alpha):
    x = x_ref[...]
    xb = x.astype(BF16)
    acc = jnp.zeros(x.shape, F32)
    for c in range(D_FF // FF_CHUNK):
        lo = c * FF_CHUNK
        g = _mm(xb, win_ref[:, lo:lo + FF_CHUNK])
        up = _mm(xb, win_ref[:, D_FF + lo:D_FF + lo + FF_CHUNK])
        hid = (jax.nn.silu(g) * up).astype(BF16)
        acc = acc + _mm(hid, wout_ref[lo // 2:(lo + FF_CHUNK) // 2, :])
    out_ref[...] = _layer_norm(alpha * x + acc, ln_g_ref[...], ln_b_ref[...])


def _compiler_params(n_grid):
    return pltpu.CompilerParams(dimension_semantics=("arbitrary",) * n_grid,
                                vmem_limit_bytes=VMEM_LIMIT_BYTES)


def _rows_scratch(rows):
    return pltpu.VMEM((rows * SUBLANES, LANES), F32)


def _halo_specs(rows, seq, time_index):
    per_tile = rows // HALO
    last = seq // HALO - 1
    main = pl.BlockSpec((None, rows, D_MODEL), lambda b, j: (b, time_index(j), 0))
    prev = pl.BlockSpec((None, HALO, D_MODEL),
                        lambda b, j: (b, jnp.maximum(time_index(j) * per_tile - 1, 0), 0))
    nxt = pl.BlockSpec((None, HALO, D_MODEL),
                       lambda b, j: (b, jnp.minimum((time_index(j) + 1) * per_tile, last), 0))
    return main, prev, nxt


def _bwd_pass(x, p):
    bsz, seq, _ = x.shape
    rows = BWD_TIME_TILE
    n_t = seq // rows
    rev = lambda j: n_t - 1 - j
    main, prev, nxt = _halo_specs(rows, seq, rev)
    vec = _const_spec((1, D_MODEL))
    return pl.pallas_call(
        _bwd_pass_kernel,
        grid=(bsz, n_t),
        in_specs=[main, prev, nxt,
                  _const_spec((D_MODEL // 2, D_MODEL)),
                  _const_spec((RNN_CONV_W, SUBLANES, LANES)), _const_spec((SUBLANES, LANES)),
                  _const_spec((N_HEADS, HEAD // 2, 2 * HEAD)), vec, vec, vec],
        out_specs=[pl.BlockSpec((None, rows, D_MODEL), lambda b, j: (b, rev(j), 0)),
                   pl.BlockSpec((None, rows * SUBLANES, LANES), lambda b, j: (b, rev(j), 0))],
        out_shape=[jax.ShapeDtypeStruct((bsz, seq, D_MODEL), F32),
                   jax.ShapeDtypeStruct((bsz, seq * SUBLANES, LANES), F32)],
        scratch_shapes=[pltpu.VMEM((rows + 2 * HALO, D_MODEL), BF16),
                        _rows_scratch(rows + 2 * HALO), _rows_scratch(rows),
                        _rows_scratch(rows), _rows_scratch(rows), _rows_scratch(1)],
        compiler_params=_compiler_params(2),
        name="lru_bwd_pass",
    )(x, x, x, p["w_xr"], p["rnn_cw"], p["rnn_cb"], p["wlru_b"], p["ba_b"], p["bx_b"], p["lam_b"])


def _fwd_pass(x, xc, hb, p, alpha):
    bsz, seq, _ = x.shape
    rows = FWD_TIME_TILE
    n_t = seq // rows
    main, prev, nxt = _halo_specs(rows, seq, lambda j: j)
    tile = pl.BlockSpec((None, rows, D_MODEL), lambda b, j: (b, j, 0))
    vec = _const_spec((1, D_MODEL))
    sq = _const_spec((D_MODEL // 2, D_MODEL))
    wide = _const_spec((D_MODEL // 2, 2 * D_MODEL))
    return pl.pallas_call(
        functools.partial(_fwd_pass_kernel, alpha=alpha),
        grid=(bsz, n_t),
        in_specs=[main, prev, nxt, tile,
                  pl.BlockSpec((None, rows * SUBLANES, LANES), lambda b, j: (b, j, 0)),
                  sq, wide, wide,
                  _const_spec((N_HEADS, HEAD // 2, 2 * HEAD)), vec, vec, vec, sq,
                  _const_spec((TAPS_PAD, D_MODEL)), vec,
                  _const_spec((2 * DFT_HALF, DFT_LEN)), _const_spec((DFT_OUT, 2 * DFT_HALF)),
                  _const_spec((2 * DFT_HALF, TAPS_PAD)),
                  vec, vec, sq,
                  _const_spec((2, D_MODEL)), sq, vec, vec],
        out_specs=tile,
        out_shape=jax.ShapeDtypeStruct((bsz, seq, D_MODEL), F32),
        scratch_shapes=[pltpu.VMEM((rows + 2 * HALO, D_MODEL), BF16),
                        pltpu.VMEM((rows + 2 * HALO, D_MODEL), F32),
                        pltpu.VMEM((2 * DFT_HALF, D_MODEL), F32),
                        _rows_scratch(rows), _rows_scratch(rows), _rows_scratch(rows),
                        _rows_scratch(1)],
        compiler_params=_compiler_params(2),
        name="mixer_fwd_pass",
    )(x, x, x, xc, hb,
      p["w_grnn"], p["w_conf"], p["w_gate"],
      p["wlru_f"], p["ba_f"], p["bx_f"], p["lam_f"], p["w_rnn_proj"],
      p["conf_cw"], p["conf_cb"], p["dft_fwd"], p["dft_inv"], p["dft_taps"],
      p["conf_ng"], p["conf_nb"], p["w_conf_proj"],
      p["gate_b"], p["w_out"], p["ln1_g"], p["ln1_b"])


def _ffn_pass(x, p, alpha):
    bsz, seq, _ = x.shape
    x2 = x.reshape(bsz * seq, D_MODEL)
    rows = FFN_ROWS
    tile = pl.BlockSpec((rows, D_MODEL), lambda i: (i, 0))
    vec = _const_spec((1, D_MODEL))
    out = pl.pallas_call(
        functools.partial(_ffn_kernel, alpha=alpha),
        grid=(bsz * seq // rows,),
        in_specs=[tile, _const_spec((D_MODEL // 2, 2 * D_FF)), _const_spec((D_FF // 2, D_MODEL)),
                  vec, vec],
        out_specs=tile,
        out_shape=jax.ShapeDtypeStruct((bsz * seq, D_MODEL), F32),
        compiler_params=_compiler_params(1),
        name="ffn_pass",
    )(x2, p["w_ffn_in"], p["w_ffn_out"], p["ln2_g"], p["ln2_b"])
    return out.reshape(bsz, seq, D_MODEL)


def _layer_params(l, w_in, rnn_conv_w, rnn_conv_b, lru_wa, lru_ba, lru_wx, lru_bx, lru_lambda,
                  w_rnn_proj, conf_conv_w, conf_conv_b, conf_norm_g, conf_norm_b, w_conf_proj,
                  gate_b, w_out, ln1_g, ln1_b, w_ffn_in, w_ffn_out, ln2_g, ln2_b):
    d = D_MODEL
    wi = _pack_rows(w_in[l])
    vec = lambda v: v.reshape(1, d).astype(F32)
    rows = lambda v: v.reshape(v.shape[:-1] + (SUBLANES, LANES)).astype(F32)
    wlru = _pack_rows(jnp.concatenate([lru_wa[l], lru_wx[l]], axis=-1))
    w_conf = wi[:, 2 * d:4 * d].reshape(d // 2, 2, LANE_BLOCKS, LANES)
    w_conf = jnp.swapaxes(w_conf, 1, 2).reshape(d // 2, 2 * d)
    dft_fwd, dft_inv, dft_taps = _dft_matrices()
    return dict(
        w_xr=wi[:, :d], w_grnn=wi[:, d:2 * d], w_conf=w_conf, w_gate=wi[:, 4 * d:],
        rnn_cw=rows(rnn_conv_w[l]), rnn_cb=rows(rnn_conv_b[l]),
        wlru_f=wlru[0], wlru_b=wlru[1],
        ba_f=vec(lru_ba[l, 0]), ba_b=vec(lru_ba[l, 1]),
        bx_f=vec(lru_bx[l, 0]), bx_b=vec(lru_bx[l, 1]),
        lam_f=vec(lru_lambda[l, 0]), lam_b=vec(lru_lambda[l, 1]),
        w_rnn_proj=_pack_rows(w_rnn_proj[l]),
        conf_cw=jnp.pad(conf_conv_w[l].astype(F32), ((0, TAPS_PAD - CONF_CONV_W), (0, 0))),
        conf_cb=vec(conf_conv_b[l]),
        dft_fwd=dft_fwd, dft_inv=dft_inv, dft_taps=dft_taps,
        conf_ng=vec(conf_norm_g[l]), conf_nb=vec(conf_norm_b[l]),
        w_conf_proj=_pack_rows(w_conf_proj[l]),
        gate_b=gate_b[l].astype(F32), w_out=_pack_rows(w_out[l]),
        ln1_g=vec(ln1_g[l]), ln1_b=vec(ln1_b[l]),
        w_ffn_in=_pack_rows(w_ffn_in[l]), w_ffn_out=_pack_rows(w_ffn_out[l]),
        ln2_g=vec(ln2_g[l]), ln2_b=vec(ln2_b[l]),
    )


def _trunk(x, layers, alpha):
    for p in layers:
        xc, hb = _bwd_pass(x, p)
        x = _fwd_pass(x, xc, hb, p, alpha)
        x = _ffn_pass(x, p, alpha)
    return x


def kernel(x_prompt, x_sample, w_in, rnn_conv_w, rnn_conv_b, lru_wa, lru_ba, lru_wx, lru_bx, lru_lambda, w_rnn_proj, conf_conv_w, conf_conv_b, conf_norm_g, conf_norm_b, w_conf_proj, gate_b, w_out, ln1_g, ln1_b, w_ffn_in, w_ffn_out, ln2_g, ln2_b):
    weights = (w_in, rnn_conv_w, rnn_conv_b, lru_wa, lru_ba, lru_wx, lru_bx, lru_lambda,
               w_rnn_proj, conf_conv_w, conf_conv_b, conf_norm_g, conf_norm_b, w_conf_proj,
               gate_b, w_out, ln1_g, ln1_b, w_ffn_in, w_ffn_out, ln2_g, ln2_b)
    depth = w_in.shape[0]
    alpha = (2.0 * depth) ** 0.25
    layers = [_layer_params(l, *weights) for l in range(depth)]
    return (_trunk(x_prompt, layers, alpha), _trunk(x_sample, layers, alpha))
```

```python
import functools

import jax
import jax.numpy as jnp
import numpy as np
from jax import lax
from jax.experimental import pallas as pl
from jax.experimental.pallas import tpu as pltpu

D_MODEL = 1024
N_HEADS = 8
HEAD = D_MODEL // N_HEADS
RNN_CONV_W = 4
CONF_CONV_W = 31
LRU_C = 8.0
LOG2_E = 1.4426950408889634
LN_EPS = 1e-5
D_FF = 2816
FF_CHUNK = 256

SUBLANES = 8
LANES = 128
LANE_BLOCKS = D_MODEL // LANES

HALO = 16
BWD_TIME_TILE = 512
FWD_TIME_TILE = 512
FFN_ROWS = 512
PACK_ROWS = 256
VMEM_LIMIT_BYTES = 56 * 1024 * 1024
F32_TINY = 1e-30
BWD_SCAN_GROUP = 8
FWD_SCAN_GROUP = 1

DFT_OUT = 128
DFT_LEN = DFT_OUT + 2 * HALO
DFT_BINS = DFT_LEN // 2 + 1
DFT_HALF = 88
TAPS_PAD = 128

F32 = jnp.float32
BF16 = jnp.bfloat16


def _const_spec(shape):
    n = len(shape)
    return pl.BlockSpec(shape, lambda *_: (0,) * n, pipeline_mode=pl.Buffered(1))


def _to_rows(dst_ref, val):
    rows = val.shape[0]
    for c in range(LANE_BLOCKS):
        dst_ref[pl.ds(c, rows, stride=LANE_BLOCKS), :] = val[:, c * LANES:(c + 1) * LANES]


def _from_rows(src_ref, rows):
    cols = [src_ref[pl.ds(c, rows, stride=LANE_BLOCKS), :] for c in range(LANE_BLOCKS)]
    return jnp.concatenate(cols, axis=1)


def _row(ref, t):
    return ref[t * SUBLANES:(t + 1) * SUBLANES, :]


def _set_row(ref, t, val):
    ref[t * SUBLANES:(t + 1) * SUBLANES, :] = val


def _depthwise_conv_rows(src_ref, w_ref, b_ref, dst_ref, n_out, width, src_offset, block):
    bias = b_ref[...]
    for t0 in range(0, n_out, block):
        accs = [bias] * block
        for k in range(width):
            wk = w_ref[k]
            for tt in range(block):
                accs[tt] = accs[tt] + wk * _row(src_ref, t0 + tt + src_offset + k)
        for tt in range(block):
            _set_row(dst_ref, t0 + tt, accs[tt])


def _pack_kernel(w_ref, out_ref):
    out_ref[...] = pltpu.bitcast(w_ref[...].astype(BF16), jnp.uint32)


def _pack_rows(w):
    lead, (k, n) = w.shape[:-2], w.shape[-2:]
    rows = int(np.prod(lead)) * k
    packed = pl.pallas_call(
        _pack_kernel,
        grid=(rows // PACK_ROWS,),
        in_specs=[pl.BlockSpec((PACK_ROWS, n), lambda i: (i, 0))],
        out_specs=pl.BlockSpec((PACK_ROWS // 2, n), lambda i: (i, 0)),
        out_shape=jax.ShapeDtypeStruct((rows // 2, n), jnp.uint32),
        compiler_params=_compiler_params(1),
        name="pack_weight",
    )(w.reshape(rows, n))
    return packed.reshape(lead + (k // 2, n))


def _mm(lhs, w_words):
    return jnp.dot(lhs, pltpu.bitcast(w_words, BF16), preferred_element_type=F32)


def _dft_matrices():
    n = np.arange(DFT_LEN)
    f = np.arange(DFT_BINS)
    ang = 2.0 * np.pi * np.outer(f, n) / DFT_LEN
    fwd = np.zeros((2 * DFT_HALF, DFT_LEN))
    fwd[:DFT_BINS] = np.cos(ang)
    fwd[DFT_HALF:DFT_HALF + DFT_BINS] = -np.sin(ang)
    i = np.arange(DFT_OUT)
    ang_i = 2.0 * np.pi * np.outer(i, f) / DFT_LEN
    c = np.full(DFT_BINS, 2.0)
    c[0] = c[-1] = 1.0
    inv = np.zeros((DFT_OUT, 2 * DFT_HALF))
    inv[:, :DFT_BINS] = c * np.cos(ang_i) / DFT_LEN
    inv[:, DFT_HALF:DFT_HALF + DFT_BINS] = -c * np.sin(ang_i) / DFT_LEN
    k = np.arange(CONF_CONV_W)
    ang_k = 2.0 * np.pi * np.outer(f, 1 + k) / DFT_LEN
    taps = np.zeros((2 * DFT_HALF, TAPS_PAD))
    taps[:DFT_BINS, :CONF_CONV_W] = np.cos(ang_k)
    taps[DFT_HALF:DFT_HALF + DFT_BINS, :CONF_CONV_W] = np.sin(ang_k)
    return jnp.asarray(fwd, BF16), jnp.asarray(inv, BF16), jnp.asarray(taps, F32)


def _softplus(z):
    return jnp.maximum(z, 0.0) + jnp.log1p(jnp.exp(-jnp.abs(z)))


def _lru_gates(xc, wg_ref, ba_ref, bx_ref, lam_ref, a_rows, b_rows):
    rows = xc.shape[0]
    for h in range(N_HEADS):
        sl = slice(h * HEAD, (h + 1) * HEAD)
        xh = xc[:, sl]
        z = _mm(xh.astype(BF16), wg_ref[h])
        r = jax.nn.sigmoid(z[:, :HEAD] + ba_ref[:, sl])
        i = jax.nn.sigmoid(z[:, HEAD:] + bx_ref[:, sl])
        rate = (-LRU_C * LOG2_E) * _softplus(-lam_ref[:, sl])
        a = jnp.exp2(r * rate)
        v = 1.0 - a * a
        b = (v * lax.rsqrt(jnp.maximum(v, F32_TINY))) * (i * xh)
        a_rows[pl.ds(h, rows, stride=LANE_BLOCKS), :] = a
        b_rows[pl.ds(h, rows, stride=LANE_BLOCKS), :] = b


def _scan_rows(a_rows, b_rows, h, order, group, emit):
    for g0 in range(0, len(order), group):
        coef_a = coef_b = None
        for t in order[g0:g0 + group]:
            a, b = _row(a_rows, t), _row(b_rows, t)
            if coef_a is None:
                coef_a, coef_b = a, b
            else:
                coef_a, coef_b = a * coef_a, a * coef_b + b
            h_t = coef_a * h + coef_b
            emit(t, h_t)
        h = h_t
    return h


def _layer_norm(x, g, b):
    mu = jnp.mean(x, axis=-1, keepdims=True)
    xm = x - mu
    var = jnp.mean(xm * xm, axis=-1, keepdims=True)
    return xm * lax.rsqrt(var + LN_EPS) * g + b


def _halo_tile(xs_ref, x_ref, prev_ref, next_ref, is_first, is_last):
    rows = x_ref.shape[0]
    xs_ref[0:HALO, :] = jnp.where(is_first, 0.0, prev_ref[...]).astype(BF16)
    xs_ref[HALO:HALO + rows, :] = x_ref[...].astype(BF16)
    xs_ref[HALO + rows:, :] = jnp.where(is_last, 0.0, next_ref[...]).astype(BF16)


def _bwd_pass_kernel(x_ref, xprev_ref, xnext_ref, wxr_ref, cw_ref, cb_ref, wg_ref, ba_ref, bx_ref,
                     lam_ref, xc_ref, hb_ref, xs, xr_rows, xc_rows, a_rows, b_rows, h_state):
    j = pl.program_id(1)
    n_t = pl.num_programs(1)
    jr = n_t - 1 - j
    rows = x_ref.shape[0]

    _halo_tile(xs, x_ref, xprev_ref, xnext_ref, jr == 0, jr == n_t - 1)
    _to_rows(xr_rows, _mm(xs[...], wxr_ref[...]))
    pad_left = RNN_CONV_W // 2
    _depthwise_conv_rows(xr_rows, cw_ref, cb_ref, xc_rows, rows, RNN_CONV_W, HALO - pad_left, 8)
    xc = _from_rows(xc_rows, rows)
    xc_ref[...] = xc

    _lru_gates(xc, wg_ref, ba_ref, bx_ref, lam_ref, a_rows, b_rows)

    h0 = jnp.where(j == 0, 0.0, h_state[...])
    h_state[...] = _scan_rows(a_rows, b_rows, h0, list(range(rows - 1, -1, -1)), BWD_SCAN_GROUP,
                              lambda t, h: _set_row(hb_ref, t, h))


def _fwd_pass_kernel(x_ref, xprev_ref, xnext_ref, xc_ref, hb_ref,
                     wg_rnn_ref, wconf_ref, wgate_ref,
                     wlru_ref, ba_ref, bx_ref, lam_ref, wrp_ref,
                     ccw_ref, ccb_ref, dft_fwd_ref, dft_inv_ref, dft_taps_ref, cng_ref, cnb_ref, wcp_ref,
                     gb_ref, wout_ref, ln_g_ref, ln_b_ref,
                     out_ref, xs, u_s, w_spec, a_rows, b_rows, h_rows, h_state, *, alpha):
    j = pl.program_id(1)
    n_t = pl.num_programs(1)
    rows = x_ref.shape[0]

    @pl.when(j == 0)
    def _():
        w_spec[...] = jnp.dot(dft_taps_ref[...], ccw_ref[...], preferred_element_type=F32,
                              precision=lax.Precision.HIGHEST)

    _halo_tile(xs, x_ref, xprev_ref, xnext_ref, j == 0, j == n_t - 1)
    xb = xs[HALO:HALO + rows, :]

    for c in range(LANE_BLOCKS):
        uv = _mm(xs[...], wconf_ref[:, 2 * LANES * c:2 * LANES * (c + 1)])
        u_s[:, c * LANES:(c + 1) * LANES] = uv[:, :LANES] * jax.nn.sigmoid(uv[:, LANES:])

    g_re, g_im = w_spec[0:DFT_HALF, :], w_spec[DFT_HALF:, :]
    y_blocks = []
    for blk in range(rows // DFT_OUT):
        ub = u_s[blk * DFT_OUT:blk * DFT_OUT + DFT_LEN, :].astype(BF16)
        spec = jnp.dot(dft_fwd_ref[...], ub, preferred_element_type=F32)
        s_re, s_im = spec[0:DFT_HALF], spec[DFT_HALF:]
        prod = jnp.concatenate([s_re * g_re - s_im * g_im, s_re * g_im + s_im * g_re], axis=0)
        y_blocks.append(jnp.dot(dft_inv_ref[...], prod.astype(BF16), preferred_element_type=F32))
    y = jnp.concatenate(y_blocks, axis=0) + ccb_ref[...]
    un = jax.nn.silu(_layer_norm(y, cng_ref[...], cnb_ref[...]))
    y_b = _mm(un.astype(BF16), wcp_ref[...])

    xc = xc_ref[...]
    _lru_gates(xc, wlru_ref, ba_ref, bx_ref, lam_ref, a_rows, b_rows)
    h0 = jnp.where(j == 0, 0.0, h_state[...])
    h_state[...] = _scan_rows(a_rows, b_rows, h0, list(range(rows)), FWD_SCAN_GROUP,
                              lambda t, h: _set_row(h_rows, t, h + _row(hb_ref, t)))
    h_sum = _from_rows(h_rows, rows)
    g_rnn = _mm(xb, wg_rnn_ref[...])
    y_a = _mm((h_sum * jax.nn.gelu(g_rnn)).astype(BF16), wrp_ref[...])

    gl = _mm(xb, wgate_ref[...])
    g0 = jax.nn.sigmoid(gl[:, :D_MODEL] + gb_ref[0:1, :])
    g1 = jax.nn.sigmoid(gl[:, D_MODEL:] + gb_ref[1:2, :])
    merged = g0 * y_a + g1 * y_b
    mo = _mm(merged.astype(BF16), wout_ref[...])
    out_ref[...] = _layer_norm(alpha * x_ref[...] + mo, ln_g_ref[...], ln_b_ref[...])


def _ffn_kernel(x_ref, win_ref, wout_ref, ln_g_ref, ln_b_ref, out_ref, *, alpha):
    x = x_ref[...]
    xb = x.astype(BF16)
    acc = jnp.zeros(x.shape, F32)
    for c in range(D_FF // FF_CHUNK):
        lo = c * FF_CHUNK
        g = _mm(xb, win_ref[:, lo:lo + FF_CHUNK])
        up = _mm(xb, win_ref[:, D_FF + lo:D_FF + lo + FF_CHUNK])
        hid = (jax.nn.silu(g) * up).astype(BF16)
        acc = acc + _mm(hid, wout_ref[lo // 2:(lo + FF_CHUNK) // 2, :])
    out_ref[...] = _layer_norm(alpha * x + acc, ln_g_ref[...], ln_b_ref[...])


def _compiler_params(n_grid):
    return pltpu.CompilerParams(dimension_semantics=("arbitrary",) * n_grid,
                                vmem_limit_bytes=VMEM_LIMIT_BYTES)


def _rows_scratch(rows):
    return pltpu.VMEM((rows * SUBLANES, LANES), F32)


def _halo_specs(rows, seq, time_index):
    per_tile = rows // HALO
    last = seq // HALO - 1
    main = pl.BlockSpec((None, rows, D_MODEL), lambda b, j: (b, time_index(j), 0))
    prev = pl.BlockSpec((None, HALO, D_MODEL),
                        lambda b, j: (b, jnp.maximum(time_index(j) * per_tile - 1, 0), 0))
    nxt = pl.BlockSpec((None, HALO, D_MODEL),
                       lambda b, j: (b, jnp.minimum((time_index(j) + 1) * per_tile, last), 0))
    return main, prev, nxt


def _bwd_pass(x, p):
    bsz, seq, _ = x.shape
    rows = BWD_TIME_TILE
    n_t = seq // rows
    rev = lambda j: n_t - 1 - j
    main, prev, nxt = _halo_specs(rows, seq, rev)
    vec = _const_spec((1, D_MODEL))
    return pl.pallas_call(
        _bwd_pass_kernel,
        grid=(bsz, n_t),
        in_specs=[main, prev, nxt,
                  _const_spec((D_MODEL // 2, D_MODEL)),
                  _const_spec((RNN_CONV_W, SUBLANES, LANES)), _const_spec((SUBLANES, LANES)),
                  _const_spec((N_HEADS, HEAD // 2, 2 * HEAD)), vec, vec, vec],
        out_specs=[pl.BlockSpec((None, rows, D_MODEL), lambda b, j: (b, rev(j), 0)),
                   pl.BlockSpec((None, rows * SUBLANES, LANES), lambda b, j: (b, rev(j), 0))],
        out_shape=[jax.ShapeDtypeStruct((bsz, seq, D_MODEL), F32),
                   jax.ShapeDtypeStruct((bsz, seq * SUBLANES, LANES), F32)],
        scratch_shapes=[pltpu.VMEM((rows + 2 * HALO, D_MODEL), BF16),
                        _rows_scratch(rows + 2 * HALO), _rows_scratch(rows),
                        _rows_scratch(rows), _rows_scratch(rows), _rows_scratch(1)],
        compiler_params=_compiler_params(2),
        name="lru_bwd_pass",
    )(x, x, x, p["w_xr"], p["rnn_cw"], p["rnn_cb"], p["wlru_b"], p["ba_b"], p["bx_b"], p["lam_b"])


def _fwd_pass(x, xc, hb, p, alpha):
    bsz, seq, _ = x.shape
    rows = FWD_TIME_TILE
    n_t = seq // rows
    main, prev, nxt = _halo_specs(rows, seq, lambda j: j)
    tile = pl.BlockSpec((None, rows, D_MODEL), lambda b, j: (b, j, 0))
    vec = _const_spec((1, D_MODEL))
    sq = _const_spec((D_MODEL // 2, D_MODEL))
    wide = _const_spec((D_MODEL // 2, 2 * D_MODEL))
    return pl.pallas_call(
        functools.partial(_fwd_pass_kernel, alpha=alpha),
        grid=(bsz, n_t),
        in_specs=[main, prev, nxt, tile,
                  pl.BlockSpec((None, rows * SUBLANES, LANES), lambda b, j: (b, j, 0)),
                  sq, wide, wide,
                  _const_spec((N_HEADS, HEAD // 2, 2 * HEAD)), vec, vec, vec, sq,
                  _const_spec((TAPS_PAD, D_MODEL)), vec,
                  _const_spec((2 * DFT_HALF, DFT_LEN)), _const_spec((DFT_OUT, 2 * DFT_HALF)),
                  _const_spec((2 * DFT_HALF, TAPS_PAD)),
                  vec, vec, sq,
                  _const_spec((2, D_MODEL)), sq, vec, vec],
        out_specs=tile,
        out_shape=jax.ShapeDtypeStruct((bsz, seq, D_MODEL), F32),
        scratch_shapes=[pltpu.VMEM((rows + 2 * HALO, D_MODEL), BF16),
                        pltpu.VMEM((rows + 2 * HALO, D_MODEL), F32),
                        pltpu.VMEM((2 * DFT_HALF, D_MODEL), F32),
                        _rows_scratch(rows), _rows_scratch(rows), _rows_scratch(rows),
                        _rows_scratch(1)],
        compiler_params=_compiler_params(2),
        name="mixer_fwd_pass",
    )(x, x, x, xc, hb,
      p["w_grnn"], p["w_conf"], p["w_gate"],
      p["wlru_f"], p["ba_f"], p["bx_f"], p["lam_f"], p["w_rnn_proj"],
      p["conf_cw"], p["conf_cb"], p["dft_fwd"], p["dft_inv"], p["dft_taps"],
      p["conf_ng"], p["conf_nb"], p["w_conf_proj"],
      p["gate_b"], p["w_out"], p["ln1_g"], p["ln1_b"])


def _ffn_pass(x, p, alpha):
    bsz, seq, _ = x.shape
    x2 = x.reshape(bsz * seq, D_MODEL)
    rows = FFN_ROWS
    tile = pl.BlockSpec((rows, D_MODEL), lambda i: (i, 0))
    vec = _const_spec((1, D_MODEL))
    out = pl.pallas_call(
        functools.partial(_ffn_kernel, alpha=alpha),
        grid=(bsz * seq // rows,),
        in_specs=[tile, _const_spec((D_MODEL // 2, 2 * D_FF)), _const_spec((D_FF // 2, D_MODEL)),
                  vec, vec],
        out_specs=tile,
        out_shape=jax.ShapeDtypeStruct((bsz * seq, D_MODEL), F32),
        compiler_params=_compiler_params(1),
        name="ffn_pass",
    )(x2, p["w_ffn_in"], p["w_ffn_out"], p["ln2_g"], p["ln2_b"])
    return out.reshape(bsz, seq, D_MODEL)


def _layer_params(l, w_in, rnn_conv_w, rnn_conv_b, lru_wa, lru_ba, lru_wx, lru_bx, lru_lambda,
                  w_rnn_proj, conf_conv_w, conf_conv_b, conf_norm_g, conf_norm_b, w_conf_proj,
                  gate_b, w_out, ln1_g, ln1_b, w_ffn_in, w_ffn_out, ln2_g, ln2_b):
    d = D_MODEL
    wi = _pack_rows(w_in[l])
    vec = lambda v: v.reshape(1, d).astype(F32)
    rows = lambda v: v.reshape(v.shape[:-1] + (SUBLANES, LANES)).astype(F32)
    wlru = _pack_rows(jnp.concatenate([lru_wa[l], lru_wx[l]], axis=-1))
    w_conf = wi[:, 2 * d:4 * d].reshape(d // 2, 2, LANE_BLOCKS, LANES)
    w_conf = jnp.swapaxes(w_conf, 1, 2).reshape(d // 2, 2 * d)
    dft_fwd, dft_inv, dft_taps = _dft_matrices()
    return dict(
        w_xr=wi[:, :d], w_grnn=wi[:, d:2 * d], w_conf=w_conf, w_gate=wi[:, 4 * d:],
        rnn_cw=rows(rnn_conv_w[l]), rnn_cb=rows(rnn_conv_b[l]),
        wlru_f=wlru[0], wlru_b=wlru[1],
        ba_f=vec(lru_ba[l, 0]), ba_b=vec(lru_ba[l, 1]),
        bx_f=vec(lru_bx[l, 0]), bx_b=vec(lru_bx[l, 1]),
        lam_f=vec(lru_lambda[l, 0]), lam_b=vec(lru_lambda[l, 1]),
        w_rnn_proj=_pack_rows(w_rnn_proj[l]),
        conf_cw=jnp.pad(conf_conv_w[l].astype(F32), ((0, TAPS_PAD - CONF_CONV_W), (0, 0))),
        conf_cb=vec(conf_conv_b[l]),
        dft_fwd=dft_fwd, dft_inv=dft_inv, dft_taps=dft_taps,
        conf_ng=vec(conf_norm_g[l]), conf_nb=vec(conf_norm_b[l]),
        w_conf_proj=_pack_rows(w_conf_proj[l]),
        gate_b=gate_b[l].astype(F32), w_out=_pack_rows(w_out[l]),
        ln1_g=vec(ln1_g[l]), ln1_b=vec(ln1_b[l]),
        w_ffn_in=_pack_rows(w_ffn_in[l]), w_ffn_out=_pack_rows(w_ffn_out[l]),
        ln2_g=vec(ln2_g[l]), ln2_b=vec(ln2_b[l]),
    )


def _trunk(x, layers, alpha):
    for p in layers:
        xc, hb = _bwd_pass(x, p)
        x = _fwd_pass(x, xc, hb, p, alpha)
        x = _ffn_pass(x, p, alpha)
    return x


def kernel(x_prompt, x_sample, w_in, rnn_conv_w, rnn_conv_b, lru_wa, lru_ba, lru_wx, lru_bx, lru_lambda, w_rnn_proj, conf_conv_w, conf_conv_b, conf_norm_g, conf_norm_b, w_conf_proj, gate_b, w_out, ln1_g, ln1_b, w_ffn_in, w_ffn_out, ln2_g, ln2_b):
    weights = (w_in, rnn_conv_w, rnn_conv_b, lru_wa, lru_ba, lru_wx, lru_bx, lru_lambda,
               w_rnn_proj, conf_conv_w, conf_conv_b, conf_norm_g, conf_norm_b, w_conf_proj,
               gate_b, w_out, ln1_g, ln1_b, w_ffn_in, w_ffn_out, ln2_g, ln2_b)
    depth = w_in.shape[0]
    alpha = (2.0 * depth) ** 0.25
    layers = [_layer_params(l, *weights) for l in range(depth)]
    return (_trunk(x_prompt, layers, alpha), _trunk(x_sample, layers, alpha))
```

```python
import functools

import jax
import jax.numpy as jnp
import numpy as np
from jax import lax
from jax.experimental import pallas as pl
from jax.experimental.pallas import tpu as pltpu

D_MODEL = 1024
N_HEADS = 8
HEAD = D_MODEL // N_HEADS
RNN_CONV_W = 4
CONF_CONV_W = 31
LRU_C = 8.0
LOG2_E = 1.4426950408889634
LN_EPS = 1e-5
D_FF = 2816
FF_CHUNK = 256

SUBLANES = 8
LANES = 128
LANE_BLOCKS = D_MODEL // LANES

HALO = 16
BWD_TIME_TILE = 1024
FWD_TIME_TILE = 512
FFN_ROWS = 1024
PACK_ROWS = 256
VMEM_LIMIT_BYTES = 56 * 1024 * 1024
F32_TINY = 1e-30
BWD_SCAN_GROUP = 8
FWD_SCAN_GROUP = 1

DFT_OUT = 128
DFT_LEN = DFT_OUT + 2 * HALO
DFT_BINS = DFT_LEN // 2 + 1
DFT_HALF = 88
TAPS_PAD = 128

F32 = jnp.float32
BF16 = jnp.bfloat16


def _const_spec(shape):
    n = len(shape)
    return pl.BlockSpec(shape, lambda *_: (0,) * n, pipeline_mode=pl.Buffered(1))


def _to_rows(dst_ref, val):
    rows = val.shape[0]
    for c in range(LANE_BLOCKS):
        dst_ref[pl.ds(c, rows, stride=LANE_BLOCKS), :] = val[:, c * LANES:(c + 1) * LANES]


def _from_rows(src_ref, rows):
    cols = [src_ref[pl.ds(c, rows, stride=LANE_BLOCKS), :] for c in range(LANE_BLOCKS)]
    return jnp.concatenate(cols, axis=1)


def _row(ref, t):
    return ref[t * SUBLANES:(t + 1) * SUBLANES, :]


def _set_row(ref, t, val):
    ref[t * SUBLANES:(t + 1) * SUBLANES, :] = val


def _depthwise_conv_rows(src_ref, w_ref, b_ref, dst_ref, n_out, width, src_offset, block):
    bias = b_ref[...]
    for t0 in range(0, n_out, block):
        accs = [bias] * block
        for k in range(width):
            wk = w_ref[k]
            for tt in range(block):
                accs[tt] = accs[tt] + wk * _row(src_ref, t0 + tt + src_offset + k)
        for tt in range(block):
            _set_row(dst_ref, t0 + tt, accs[tt])


def _pack_kernel(w_ref, out_ref):
    out_ref[...] = pltpu.bitcast(w_ref[...].astype(BF16), jnp.uint32)


def _pack_rows(w):
    lead, (k, n) = w.shape[:-2], w.shape[-2:]
    rows = int(np.prod(lead)) * k
    packed = pl.pallas_call(
        _pack_kernel,
        grid=(rows // PACK_ROWS,),
        in_specs=[pl.BlockSpec((PACK_ROWS, n), lambda i: (i, 0))],
        out_specs=pl.BlockSpec((PACK_ROWS // 2, n), lambda i: (i, 0)),
        out_shape=jax.ShapeDtypeStruct((rows // 2, n), jnp.uint32),
        compiler_params=_compiler_params(1),
        name="pack_weight",
    )(w.reshape(rows, n))
    return packed.reshape(lead + (k // 2, n))


def _mm(lhs, w_words):
    return jnp.dot(lhs, pltpu.bitcast(w_words, BF16), preferred_element_type=F32)


def _dft_matrices():
    n = np.arange(DFT_LEN)
    f = np.arange(DFT_BINS)
    ang = 2.0 * np.pi * np.outer(f, n) / DFT_LEN
    fwd = np.zeros((2 * DFT_HALF, DFT_LEN))
    fwd[:DFT_BINS] = np.cos(ang)
    fwd[DFT_HALF:DFT_HALF + DFT_BINS] = -np.sin(ang)
    i = np.arange(DFT_OUT)
    ang_i = 2.0 * np.pi * np.outer(i, f) / DFT_LEN
    c = np.full(DFT_BINS, 2.0)
    c[0] = c[-1] = 1.0
    inv = np.zeros((DFT_OUT, 2 * DFT_HALF))
    inv[:, :DFT_BINS] = c * np.cos(ang_i) / DFT_LEN
    inv[:, DFT_HALF:DFT_HALF + DFT_BINS] = -c * np.sin(ang_i) / DFT_LEN
    k = np.arange(CONF_CONV_W)
    ang_k = 2.0 * np.pi * np.outer(f, 1 + k) / DFT_LEN
    taps = np.zeros((2 * DFT_HALF, TAPS_PAD))
    taps[:DFT_BINS, :CONF_CONV_W] = np.cos(ang_k)
    taps[DFT_HALF:DFT_HALF + DFT_BINS, :CONF_CONV_W] = np.sin(ang_k)
    return jnp.asarray(fwd, BF16), jnp.asarray(inv, BF16), jnp.asarray(taps, F32)


def _softplus(z):
    return jnp.maximum(z, 0.0) + jnp.log1p(jnp.exp(-jnp.abs(z)))


def _lru_gates(xc, wg_ref, ba_ref, bx_ref, lam_ref, a_rows, b_rows):
    rows = xc.shape[0]
    for h in range(N_HEADS):
        sl = slice(h * HEAD, (h + 1) * HEAD)
        xh = xc[:, sl]
        z = _mm(xh.astype(BF16), wg_ref[h])
        r = jax.nn.sigmoid(z[:, :HEAD] + ba_ref[:, sl])
        i = jax.nn.sigmoid(z[:, HEAD:] + bx_ref[:, sl])
        rate = (-LRU_C * LOG2_E) * _softplus(-lam_ref[:, sl])
        a = jnp.exp2(r * rate)
        v = 1.0 - a * a
        b = (v * lax.rsqrt(jnp.maximum(v, F32_TINY))) * (i * xh)
        a_rows[pl.ds(h, rows, stride=LANE_BLOCKS), :] = a
        b_rows[pl.ds(h, rows, stride=LANE_BLOCKS), :] = b


def _scan_rows(a_rows, b_rows, h, order, group, emit):
    for g0 in range(0, len(order), group):
        coef_a = coef_b = None
        for t in order[g0:g0 + group]:
            a, b = _row(a_rows, t), _row(b_rows, t)
            if coef_a is None:
                coef_a, coef_b = a, b
            else:
                coef_a, coef_b = a * coef_a, a * coef_b + b
            h_t = coef_a * h + coef_b
            emit(t, h_t)
        h = h_t
    return h


def _layer_norm(x, g, b):
    mu = jnp.mean(x, axis=-1, keepdims=True)
    xm = x - mu
    var = jnp.mean(xm * xm, axis=-1, keepdims=True)
    return xm * lax.rsqrt(var + LN_EPS) * g + b


def _halo_tile(xs_ref, x_ref, prev_ref, next_ref, is_first, is_last):
    rows = x_ref.shape[0]
    xs_ref[0:HALO, :] = jnp.where(is_first, 0.0, prev_ref[...]).astype(BF16)
    xs_ref[HALO:HALO + rows, :] = x_ref[...].astype(BF16)
    xs_ref[HALO + rows:, :] = jnp.where(is_last, 0.0, next_ref[...]).astype(BF16)


def _bwd_pass_kernel(x_ref, xprev_ref, xnext_ref, wxr_ref, cw_ref, cb_ref, wg_ref, ba_ref, bx_ref,
                     lam_ref, xc_ref, hb_ref, xs, xr_rows, xc_rows, a_rows, b_rows, h_state):
    j = pl.program_id(1)
    n_t = pl.num_programs(1)
    jr = n_t - 1 - j
    rows = x_ref.shape[0]

    _halo_tile(xs, x_ref, xprev_ref, xnext_ref, jr == 0, jr == n_t - 1)
    _to_rows(xr_rows, _mm(xs[...], wxr_ref[...]))
    pad_left = RNN_CONV_W // 2
    _depthwise_conv_rows(xr_rows, cw_ref, cb_ref, xc_rows, rows, RNN_CONV_W, HALO - pad_left, 8)
    xc = _from_rows(xc_rows, rows)
    xc_ref[...] = xc

    _lru_gates(xc, wg_ref, ba_ref, bx_ref, lam_ref, a_rows, b_rows)

    h0 = jnp.where(j == 0, 0.0, h_state[...])
    h_state[...] = _scan_rows(a_rows, b_rows, h0, list(range(rows - 1, -1, -1)), BWD_SCAN_GROUP,
                              lambda t, h: _set_row(hb_ref, t, h))


def _fwd_pass_kernel(x_ref, xprev_ref, xnext_ref, xc_ref, hb_ref,
                     wg_rnn_ref, wconf_ref, wgate_ref,
                     wlru_ref, ba_ref, bx_ref, lam_ref, wrp_ref,
                     ccw_ref, ccb_ref, dft_fwd_ref, dft_inv_ref, dft_taps_ref, cng_ref, cnb_ref, wcp_ref,
                     gb_ref, wout_ref, ln_g_ref, ln_b_ref,
                     out_ref, xs, u_s, w_spec, a_rows, b_rows, h_rows, h_state, *, alpha):
    j = pl.program_id(1)
    n_t = pl.num_programs(1)
    rows = x_ref.shape[0]

    @pl.when(j == 0)
    def _():
        w_spec[...] = jnp.dot(dft_taps_ref[...], ccw_ref[...], preferred_element_type=F32,
                              precision=lax.Precision.HIGHEST)

    _halo_tile(xs, x_ref, xprev_ref, xnext_ref, j == 0, j == n_t - 1)
    xb = xs[HALO:HALO + rows, :]

    for c in range(LANE_BLOCKS):
        uv = _mm(xs[...], wconf_ref[:, 2 * LANES * c:2 * LANES * (c + 1)])
        u_s[:, c * LANES:(c + 1) * LANES] = uv[:, :LANES] * jax.nn.sigmoid(uv[:, LANES:])

    g_re, g_im = w_spec[0:DFT_HALF, :], w_spec[DFT_HALF:, :]
    y_blocks = []
    for blk in range(rows // DFT_OUT):
        ub = u_s[blk * DFT_OUT:blk * DFT_OUT + DFT_LEN, :].astype(BF16)
        spec = jnp.dot(dft_fwd_ref[...], ub, preferred_element_type=F32)
        s_re, s_im = spec[0:DFT_HALF], spec[DFT_HALF:]
        prod = jnp.concatenate([s_re * g_re - s_im * g_im, s_re * g_im + s_im * g_re], axis=0)
        y_blocks.append(jnp.dot(dft_inv_ref[...], prod.astype(BF16), preferred_element_type=F32))
    y = jnp.concatenate(y_blocks, axis=0) + ccb_ref[...]
    un = jax.nn.silu(_layer_norm(y, cng_ref[...], cnb_ref[...]))
    y_b = _mm(un.astype(BF16), wcp_ref[...])

    xc = xc_ref[...]
    _lru_gates(xc, wlru_ref, ba_ref, bx_ref, lam_ref, a_rows, b_rows)
    h0 = jnp.where(j == 0, 0.0, h_state[...])
    h_state[...] = _scan_rows(a_rows, b_rows, h0, list(range(rows)), FWD_SCAN_GROUP,
                              lambda t, h: _set_row(h_rows, t, h + _row(hb_ref, t)))
    h_sum = _from_rows(h_rows, rows)
    g_rnn = _mm(xb, wg_rnn_ref[...])
    y_a = _mm((h_sum * jax.nn.gelu(g_rnn)).astype(BF16), wrp_ref[...])

    gl = _mm(xb, wgate_ref[...])
    g0 = jax.nn.sigmoid(gl[:, :D_MODEL] + gb_ref[0:1, :])
    g1 = jax.nn.sigmoid(gl[:, D_MODEL:] + gb_ref[1:2, :])
    merged = g0 * y_a + g1 * y_b
    mo = _mm(merged.astype(BF16), wout_ref[...])
    out_ref[...] = _layer_norm(alpha * x_ref[...] + mo, ln_g_ref[...], ln_b_ref[...])


def _ffn_kernel(x_ref, win_ref, wout_ref, ln_g_ref, ln_b_ref, out_ref, *, alpha):
    x = x_ref[...]
    xb = x.astype(BF16)
    acc = jnp.zeros(x.shape, F32)
    for c in range(D_FF // FF_CHUNK):
        lo = c * FF_CHUNK
        g = _mm(xb, win_ref[:, lo:lo + FF_CHUNK])
        up = _mm(xb, win_ref[:, D_FF + lo:D_FF + lo + FF_CHUNK])
        hid = (jax.nn.silu(g) * up).astype(BF16)
        acc = acc + _mm(hid, wout_ref[lo // 2:(lo + FF_CHUNK) // 2, :])
    out_ref[...] = _layer_norm(alpha * x + acc, ln_g_ref[...], ln_b_ref[...])


def _compiler_params(n_grid):
    return pltpu.CompilerParams(dimension_semantics=("arbitrary",) * n_grid,
                                vmem_limit_bytes=VMEM_LIMIT_BYTES)


def _rows_scratch(rows):
    return pltpu.VMEM((rows * SUBLANES, LANES), F32)


def _halo_specs(rows, seq, time_index):
    per_tile = rows // HALO
    last = seq // HALO - 1
    main = pl.BlockSpec((None, rows, D_MODEL), lambda b, j: (b, time_index(j), 0))
    prev = pl.BlockSpec((None, HALO, D_MODEL),
                        lambda b, j: (b, jnp.maximum(time_index(j) * per_tile - 1, 0), 0))
    nxt = pl.BlockSpec((None, HALO, D_MODEL),
                       lambda b, j: (b, jnp.minimum((time_index(j) + 1) * per_tile, last), 0))
    return main, prev, nxt


def _bwd_pass(x, p):
    bsz, seq, _ = x.shape
    rows = BWD_TIME_TILE
    n_t = seq // rows
    rev = lambda j: n_t - 1 - j
    main, prev, nxt = _halo_specs(rows, seq, rev)
    vec = _const_spec((1, D_MODEL))
    return pl.pallas_call(
        _bwd_pass_kernel,
        grid=(bsz, n_t),
        in_specs=[main, prev, nxt,
                  _const_spec((D_MODEL // 2, D_MODEL)),
                  _const_spec((RNN_CONV_W, SUBLANES, LANES)), _const_spec((SUBLANES, LANES)),
                  _const_spec((N_HEADS, HEAD // 2, 2 * HEAD)), vec, vec, vec],
        out_specs=[pl.BlockSpec((None, rows, D_MODEL), lambda b, j: (b, rev(j), 0)),
                   pl.BlockSpec((None, rows * SUBLANES, LANES), lambda b, j: (b, rev(j), 0))],
        out_shape=[jax.ShapeDtypeStruct((bsz, seq, D_MODEL), F32),
                   jax.ShapeDtypeStruct((bsz, seq * SUBLANES, LANES), F32)],
        scratch_shapes=[pltpu.VMEM((rows + 2 * HALO, D_MODEL), BF16),
                        _rows_scratch(rows + 2 * HALO), _rows_scratch(rows),
                        _rows_scratch(rows), _rows_scratch(rows), _rows_scratch(1)],
        compiler_params=_compiler_params(2),
        name="lru_bwd_pass",
    )(x, x, x, p["w_xr"], p["rnn_cw"], p["rnn_cb"], p["wlru_b"], p["ba_b"], p["bx_b"], p["lam_b"])


def _fwd_pass(x, xc, hb, p, alpha):
    bsz, seq, _ = x.shape
    rows = FWD_TIME_TILE
    n_t = seq // rows
    main, prev, nxt = _halo_specs(rows, seq, lambda j: j)
    tile = pl.BlockSpec((None, rows, D_MODEL), lambda b, j: (b, j, 0))
    vec = _const_spec((1, D_MODEL))
    sq = _const_spec((D_MODEL // 2, D_MODEL))
    wide = _const_spec((D_MODEL // 2, 2 * D_MODEL))
    return pl.pallas_call(
        functools.partial(_fwd_pass_kernel, alpha=alpha),
        grid=(bsz, n_t),
        in_specs=[main, prev, nxt, tile,
                  pl.BlockSpec((None, rows * SUBLANES, LANES), lambda b, j: (b, j, 0)),
                  sq, wide, wide,
                  _const_spec((N_HEADS, HEAD // 2, 2 * HEAD)), vec, vec, vec, sq,
                  _const_spec((TAPS_PAD, D_MODEL)), vec,
                  _const_spec((2 * DFT_HALF, DFT_LEN)), _const_spec((DFT_OUT, 2 * DFT_HALF)),
                  _const_spec((2 * DFT_HALF, TAPS_PAD)),
                  vec, vec, sq,
                  _const_spec((2, D_MODEL)), sq, vec, vec],
        out_specs=tile,
        out_shape=jax.ShapeDtypeStruct((bsz, seq, D_MODEL), F32),
        scratch_shapes=[pltpu.VMEM((rows + 2 * HALO, D_MODEL), BF16),
                        pltpu.VMEM((rows + 2 * HALO, D_MODEL), F32),
                        pltpu.VMEM((2 * DFT_HALF, D_MODEL), F32),
                        _rows_scratch(rows), _rows_scratch(rows), _rows_scratch(rows),
                        _rows_scratch(1)],
        compiler_params=_compiler_params(2),
        name="mixer_fwd_pass",
    )(x, x, x, xc, hb,
      p["w_grnn"], p["w_conf"], p["w_gate"],
      p["wlru_f"], p["ba_f"], p["bx_f"], p["lam_f"], p["w_rnn_proj"],
      p["conf_cw"], p["conf_cb"], p["dft_fwd"], p["dft_inv"], p["dft_taps"],
      p["conf_ng"], p["conf_nb"], p["w_conf_proj"],
      p["gate_b"], p["w_out"], p["ln1_g"], p["ln1_b"])


def _ffn_pass(x, p, alpha):
    bsz, seq, _ = x.shape
    x2 = x.reshape(bsz * seq, D_MODEL)
    rows = FFN_ROWS
    tile = pl.BlockSpec((rows, D_MODEL), lambda i: (i, 0))
    vec = _const_spec((1, D_MODEL))
    out = pl.pallas_call(
        functools.partial(_ffn_kernel, alpha=alpha),
        grid=(bsz * seq // rows,),
        in_specs=[tile, _const_spec((D_MODEL // 2, 2 * D_FF)), _const_spec((D_FF // 2, D_MODEL)),
                  vec, vec],
        out_specs=tile,
        out_shape=jax.ShapeDtypeStruct((bsz * seq, D_MODEL), F32),
        compiler_params=_compiler_params(1),
        name="ffn_pass",
    )(x2, p["w_ffn_in"], p["w_ffn_out"], p["ln2_g"], p["ln2_b"])
    return out.reshape(bsz, seq, D_MODEL)


def _layer_params(l, w_in, rnn_conv_w, rnn_conv_b, lru_wa, lru_ba, lru_wx, lru_bx, lru_lambda,
                  w_rnn_proj, conf_conv_w, conf_conv_b, conf_norm_g, conf_norm_b, w_conf_proj,
                  gate_b, w_out, ln1_g, ln1_b, w_ffn_in, w_ffn_out, ln2_g, ln2_b):
    d = D_MODEL
    wi = _pack_rows(w_in[l])
    vec = lambda v: v.reshape(1, d).astype(F32)
    rows = lambda v: v.reshape(v.shape[:-1] + (SUBLANES, LANES)).astype(F32)
    wlru = _pack_rows(jnp.concatenate([lru_wa[l], lru_wx[l]], axis=-1))
    w_conf = wi[:, 2 * d:4 * d].reshape(d // 2, 2, LANE_BLOCKS, LANES)
    w_conf = jnp.swapaxes(w_conf, 1, 2).reshape(d // 2, 2 * d)
    dft_fwd, dft_inv, dft_taps = _dft_matrices()
    return dict(
        w_xr=wi[:, :d], w_grnn=wi[:, d:2 * d], w_conf=w_conf, w_gate=wi[:, 4 * d:],
        rnn_cw=rows(rnn_conv_w[l]), rnn_cb=rows(rnn_conv_b[l]),
        wlru_f=wlru[0], wlru_b=wlru[1],
        ba_f=vec(lru_ba[l, 0]), ba_b=vec(lru_ba[l, 1]),
        bx_f=vec(lru_bx[l, 0]), bx_b=vec(lru_bx[l, 1]),
        lam_f=vec(lru_lambda[l, 0]), lam_b=vec(lru_lambda[l, 1]),
        w_rnn_proj=_pack_rows(w_rnn_proj[l]),
        conf_cw=jnp.pad(conf_conv_w[l].astype(F32), ((0, TAPS_PAD - CONF_CONV_W), (0, 0))),
        conf_cb=vec(conf_conv_b[l]),
        dft_fwd=dft_fwd, dft_inv=dft_inv, dft_taps=dft_taps,
        conf_ng=vec(conf_norm_g[l]), conf_nb=vec(conf_norm_b[l]),
        w_conf_proj=_pack_rows(w_conf_proj[l]),
        gate_b=gate_b[l].astype(F32), w_out=_pack_rows(w_out[l]),
        ln1_g=vec(ln1_g[l]), ln1_b=vec(ln1_b[l]),
        w_ffn_in=_pack_rows(w_ffn_in[l]), w_ffn_out=_pack_rows(w_ffn_out[l]),
        ln2_g=vec(ln2_g[l]), ln2_b=vec(ln2_b[l]),
    )


def _trunk(x, layers, alpha):
    for p in layers:
        xc, hb = _bwd_pass(x, p)
        x = _fwd_pass(x, xc, hb, p, alpha)
        x = _ffn_pass(x, p, alpha)
    return x


def kernel(x_prompt, x_sample, w_in, rnn_conv_w, rnn_conv_b, lru_wa, lru_ba, lru_wx, lru_bx, lru_lambda, w_rnn_proj, conf_conv_w, conf_conv_b, conf_norm_g, conf_norm_b, w_conf_proj, gate_b, w_out, ln1_g, ln1_b, w_ffn_in, w_ffn_out, ln2_g, ln2_b):
    weights = (w_in, rnn_conv_w, rnn_conv_b, lru_wa, lru_ba, lru_wx, lru_bx, lru_lambda,
               w_rnn_proj, conf_conv_w, conf_conv_b, conf_norm_g, conf_norm_b, w_conf_proj,
               gate_b, w_out, ln1_g, ln1_b, w_ffn_in, w_ffn_out, ln2_g, ln2_b)
    depth = w_in.shape[0]
    alpha = (2.0 * depth) ** 0.25
    layers = [_layer_params(l, *weights) for l in range(depth)]
    return (_trunk(x_prompt, layers, alpha), _trunk(x_sample, layers, alpha))
```
